```python
import jax
import jax.numpy as jnp
from jax import lax
import numpy as np

D_MODEL = 1024
BATCH = 32
SEQ = 256
DEPTH = 4
DEC_BATCH = 2
DEC_SEQ = 4096
PAST_LEN = 256

GRID_W = 64
N_MIXERS = 3
MIX_NA, MIX_SWA, MIX_DENSE = 0, 1, 2
N_NA_LAYERS = (DEPTH + 2) // 3
N_SWA_LAYERS = (DEPTH + 1) // 3
N_DENSE_LAYERS = DEPTH // 3
N_HEADS = 16
N_KV_HEADS = 4
HEAD_DIM = D_MODEL // N_HEADS
GQ = N_HEADS // N_KV_HEADS
Q_DIM = N_HEADS * HEAD_DIM
KV_DIM = N_KV_HEADS * HEAD_DIM
ATTN_SCALE = HEAD_DIM ** -0.5
BLOCK = 128
SWA_WINDOW = 128
NA_KH = 8
NA_KW = 16
ROPE_THETA = 10000.0
NORM_EPS = 1e-6
NEG_INF = -1e30
PEER_HEADS = 8
PEER_N_KEYS = 128
PEER_N_EXPERTS = PEER_N_KEYS * PEER_N_KEYS
PEER_TOPK = 16
PEER_DK = 128
PEER_BLOCK = 128

kernel_name = 'hybrid_diffusion_prefix_trunk_step'


def rmsnorm(x, g):
    xf = x.astype(jnp.float32)
    y = xf * lax.rsqrt(jnp.mean(xf * xf, axis=-1, keepdims=True) + NORM_EPS)
    return (y * g.astype(jnp.float32)).astype(x.dtype)


def modulate(x, g, shift, scale):
    return rmsnorm(x, g) * (1 + scale) + shift


def rope_1d(x, pos):
    half = x.shape[-1] // 2
    freqs = ROPE_THETA ** (-jnp.arange(half, dtype=jnp.float32) / half)
    ang = pos.astype(jnp.float32)[:, None] * freqs[None, :]
    cos = jnp.cos(ang)[:, None, :].astype(x.dtype)
    sin = jnp.sin(ang)[:, None, :].astype(x.dtype)
    x1, x2 = x[..., :half], x[..., half:]
    return jnp.concatenate([x1 * cos - x2 * sin, x1 * sin + x2 * cos], axis=-1)


def rope_axial(x):
    t = jnp.arange(x.shape[1])
    half = x.shape[-1] // 2
    return jnp.concatenate([rope_1d(x[..., :half], t // GRID_W), rope_1d(x[..., half:], t % GRID_W)], axis=-1)


def project_qkv(h, w):
    b, t, _ = h.shape
    qkv = h @ w
    q = qkv[..., :Q_DIM].reshape(b, t, N_HEADS, HEAD_DIM)
    k = qkv[..., Q_DIM:Q_DIM + KV_DIM].reshape(b, t, N_KV_HEADS, HEAD_DIM)
    v = qkv[..., Q_DIM + KV_DIM:].reshape(b, t, N_KV_HEADS, HEAD_DIM)
    return q, k, v


def group_heads(q):
    b, t = q.shape[:2]
    return q.reshape(b, t, N_KV_HEADS, GQ, HEAD_DIM)


def sink_column(sink, lead_shape):
    col = sink.astype(jnp.float32).reshape(N_KV_HEADS, GQ, 1, 1)
    return jnp.broadcast_to(col, lead_shape + (1,))


def context_attention(q, k, v, sink):
    b, l = q.shape[:2]
    s = jnp.einsum('bqhgd,bkhd->bhgqk', q, k).astype(jnp.float32) * ATTN_SCALE
    if sink is not None:
        s = jnp.concatenate([s, sink_column(sink, s.shape[:-1])], axis=-1)
    p = jax.nn.softmax(s, axis=-1)[..., :l].astype(v.dtype)
    o = jnp.einsum('bhgqk,bkhd->bqhgd', p, v)
    return o.reshape(b, l, Q_DIM)


def neighbourhood_attention(q, k, v, ck, cv, rpb):
    b, s = q.shape[:2]
    rows = s // GRID_W
    kh = min(NA_KH, rows)
    qg = q.reshape(b, rows, GRID_W, N_KV_HEADS, GQ, HEAD_DIM)
    kg = k.reshape(b, rows, GRID_W, N_KV_HEADS, HEAD_DIM)
    vg = v.reshape(b, rows, GRID_W, N_KV_HEADS, HEAD_DIM)
    col = jnp.arange(GRID_W)
    col_start = jnp.clip(col - NA_KW // 2, 0, GRID_W - NA_KW)
    col_idx = col_start[:, None] + jnp.arange(NA_KW)[None, :]
    dc = col_idx - col[:, None] + (NA_KW - 1)

    def row_block(r):
        rs = jnp.clip(r - kh // 2, 0, rows - kh)
        q_r = lax.dynamic_index_in_dim(qg, r, axis=1, keepdims=False)
        k_win = lax.dynamic_slice_in_dim(kg, rs, kh, axis=1)[:, :, col_idx]
        v_win = lax.dynamic_slice_in_dim(vg, rs, kh, axis=1)[:, :, col_idx]
        s_loc = jnp.einsum('bqhgd,brqwhd->bhgqrw', q_r, k_win).astype(jnp.float32) * ATTN_SCALE
        dr = rs + jnp.arange(kh) - r + (NA_KH - 1)
        bias = rpb[:, dr][:, :, dc].reshape(N_KV_HEADS, GQ, kh, GRID_W, NA_KW)
        s_loc = s_loc + jnp.transpose(bias, (0, 1, 3, 2, 4)).astype(jnp.float32)
        s_ctx = jnp.einsum('bqhgd,blhd->bhgql', q_r, ck).astype(jnp.float32) * ATTN_SCALE
        logits = jnp.concatenate([s_loc.reshape(b, N_KV_HEADS, GQ, GRID_W, kh * NA_KW), s_ctx], axis=-1)
        p = jax.nn.softmax(logits, axis=-1).astype(v.dtype)
        p_loc = p[..., :kh * NA_KW].reshape(b, N_KV_HEADS, GQ, GRID_W, kh, NA_KW)
        p_ctx = p[..., kh * NA_KW:]
        return jnp.einsum('bhgqrw,brqwhd->bqhgd', p_loc, v_win) + jnp.einsum('bhgql,blhd->bqhgd', p_ctx, cv)

    o = lax.map(row_block, jnp.arange(rows))
    return jnp.moveaxis(o, 0, 1).reshape(b, s, Q_DIM)


def sliding_window_attention(q, k, v, ck, cv, sink):
    b, s = q.shape[:2]
    nb = s // BLOCK
    qb = q.reshape(b, nb, BLOCK, N_KV_HEADS, GQ, HEAD_DIM)
    pad = ((0, 0), (BLOCK, BLOCK), (0, 0), (0, 0))
    kp = jnp.pad(k, pad).reshape(b, nb + 2, BLOCK, N_KV_HEADS, HEAD_DIM)
    vp = jnp.pad(v, pad).reshape(b, nb + 2, BLOCK, N_KV_HEADS, HEAD_DIM)
    k_band = jnp.concatenate([kp[:, :-2], kp[:, 1:-1], kp[:, 2:]], axis=2)
    v_band = jnp.concatenate([vp[:, :-2], vp[:, 1:-1], vp[:, 2:]], axis=2)
    s_loc = jnp.einsum('bnqhgd,bnkhd->bnhgqk', qb, k_band).astype(jnp.float32) * ATTN_SCALE
    q_pos = jnp.arange(nb)[:, None, None] * BLOCK + jnp.arange(BLOCK)[None, :, None]
    k_pos = jnp.arange(nb)[:, None, None] * BLOCK - BLOCK + jnp.arange(3 * BLOCK)[None, None, :]
    mask = (jnp.abs(q_pos - k_pos) <= SWA_WINDOW) & (k_pos >= 0) & (k_pos < s)
    s_loc = jnp.where(mask[None, :, None, None], s_loc, NEG_INF)
    s_ctx = jnp.einsum('bnqhgd,blhd->bnhgql', qb, ck).astype(jnp.float32) * ATTN_SCALE
    logits = jnp.concatenate([s_loc, s_ctx, sink_column(sink, s_ctx.shape[:-1])], axis=-1)
    p = jax.nn.softmax(logits, axis=-1).astype(v.dtype)
    nk = 3 * BLOCK
    nc = ck.shape[1]
    o = (jnp.einsum('bnhgqk,bnkhd->bnqhgd', p[..., :nk], v_band)
         + jnp.einsum('bnhgql,blhd->bnqhgd', p[..., nk:nk + nc], cv))
    return o.reshape(b, s, Q_DIM)


def dense_block_attention(q, k, v, ck, cv):
    b, s = q.shape[:2]
    nb = s // BLOCK
    qb = jnp.moveaxis(q.reshape(b, nb, BLOCK, N_KV_HEADS, GQ, HEAD_DIM), 1, 0)

    def q_block(qi):
        s_lat = jnp.einsum('bqhgd,bkhd->bhgqk', qi, k).astype(jnp.float32) * ATTN_SCALE
        s_ctx = jnp.einsum('bqhgd,blhd->bhgql', qi, ck).astype(jnp.float32) * ATTN_SCALE
        p = jax.nn.softmax(jnp.concatenate([s_lat, s_ctx], axis=-1), axis=-1).astype(v.dtype)
        return (jnp.einsum('bhgqk,bkhd->bqhgd', p[..., :s], v)
                + jnp.einsum('bhgql,blhd->bqhgd', p[..., s:], cv))

    o = lax.map(q_block, qb)
    return jnp.moveaxis(o, 0, 1).reshape(b, s, Q_DIM)


def peer_ffn(x, wq, keys, u, v):
    b, t, d = x.shape
    n = b * t
    xt = x.reshape(n, d)
    q = (xt @ wq).reshape(n, PEER_HEADS, 2, PEER_DK)
    s = jnp.einsum('nhpd,hpkd->nhpk', q, keys).astype(jnp.float32)
    s1, i1 = lax.top_k(s[:, :, 0], PEER_TOPK)
    s2, i2 = lax.top_k(s[:, :, 1], PEER_TOPK)
    cand = (s1[..., :, None] + s2[..., None, :]).reshape(n, PEER_HEADS, PEER_TOPK * PEER_TOPK)
    top_s, pos = lax.top_k(cand, PEER_TOPK)
    idx = (jnp.take_along_axis(i1, pos // PEER_TOPK, axis=-1) * PEER_N_KEYS
           + jnp.take_along_axis(i2, pos % PEER_TOPK, axis=-1))
    gate = jax.nn.softmax(top_s, axis=-1)
    nblk = n // PEER_BLOCK

    def expert_block(args):
        xi, ii, gi = args
        ue = jnp.take(u, ii, axis=0)
        ve = jnp.take(v, ii, axis=0)
        a = jax.nn.gelu(jnp.einsum('nd,nhkd->nhk', xi, ue).astype(jnp.float32), approximate=False) * gi
        return jnp.einsum('nhk,nhkd->nd', a.astype(xi.dtype), ve)

    out = lax.map(expert_block, (xt.reshape(nblk, PEER_BLOCK, d),
                                 idx.reshape(nblk, PEER_BLOCK, PEER_HEADS, PEER_TOPK),
                                 gate.reshape(nblk, PEER_BLOCK, PEER_HEADS, PEER_TOPK)))
    return out.reshape(b, t, d)


def setup_inputs(seed: int = 0) -> dict:
    key = jax.random.key(seed)
    ks = jax.random.split(key, 20)

    def nrm(k, shape, scale):
        return jax.random.normal(k, shape, jnp.float32) * scale

    cache_shape = (DEC_BATCH, DEPTH, PAST_LEN, N_KV_HEADS, HEAD_DIM)
    return {
        'x_prompt': nrm(ks[0], (BATCH, SEQ, D_MODEL), 1.0),
        'x_sample': nrm(ks[1], (DEC_BATCH, DEC_SEQ, D_MODEL), 1.0),
        'cache_k': nrm(ks[2], cache_shape, 1.0),
        'cache_v': nrm(ks[3], cache_shape, 1.0),
        'c': nrm(ks[4], (DEC_BATCH, D_MODEL), 1.0),
        'c_ctx': nrm(ks[5], (D_MODEL,), 1.0),
        'w_ada': nrm(ks[6], (DEPTH, D_MODEL, 6 * D_MODEL), 0.5 * D_MODEL ** -0.5),
        'b_ada': nrm(ks[7], (DEPTH, 6 * D_MODEL), 0.02),
        'norm_g': 1.0 + nrm(ks[8], (DEPTH, 2, D_MODEL), 0.05),
        'final_g': 1.0 + nrm(ks[9], (D_MODEL,), 0.05),
        'w_qkv': nrm(ks[10], (DEPTH, D_MODEL, Q_DIM + 2 * KV_DIM), D_MODEL ** -0.5),
        'w_o': nrm(ks[11], (DEPTH, Q_DIM, D_MODEL), Q_DIM ** -0.5),
        'na_rpb': nrm(ks[12], (N_NA_LAYERS, N_HEADS, 2 * NA_KH - 1, 2 * NA_KW - 1), 0.1),
        'swa_sink': nrm(ks[13], (N_SWA_LAYERS, N_HEADS), 0.5),
        'qk_gain': 1.0 + nrm(ks[14], (N_DENSE_LAYERS, 2, HEAD_DIM), 0.05),
        'peer_wq': nrm(ks[15], (DEPTH, D_MODEL, PEER_HEADS * 2 * PEER_DK), D_MODEL ** -0.5),
        'peer_keys': nrm(ks[16], (DEPTH, PEER_HEADS, 2, PEER_N_KEYS, PEER_DK), PEER_DK ** -0.5),
        'peer_u': nrm(ks[17], (DEPTH, PEER_N_EXPERTS, D_MODEL), D_MODEL ** -0.5),
        'peer_v': nrm(ks[18], (DEPTH, PEER_N_EXPERTS, D_MODEL), 0.3),
    }


def reference(x_prompt, x_sample, cache_k, cache_v, c, c_ctx, w_ada, b_ada, norm_g, final_g,
              w_qkv, w_o, na_rpb, swa_sink, qk_gain, peer_wq, peer_keys, peer_u, peer_v):
    x = x_prompt
    ks, vs = [], []
    for l in range(DEPTH):
        kind, j = l % N_MIXERS, l // N_MIXERS
        sh1, sc1, g1, sh2, sc2, g2 = jnp.split(jax.nn.silu(c_ctx) @ w_ada[l] + b_ada[l], 6, axis=-1)
        q, k, v = project_qkv(modulate(x, norm_g[l, 0], sh1, sc1), w_qkv[l])
        if kind == MIX_DENSE:
            q = rmsnorm(q, qk_gain[j, 0])
            k = rmsnorm(k, qk_gain[j, 1])
        sink = swa_sink[j] if kind == MIX_SWA else None
        o = context_attention(group_heads(q), k, v, sink)
        x = x + g1 * (o @ w_o[l])
        x = x + g2 * peer_ffn(modulate(x, norm_g[l, 1], sh2, sc2), peer_wq[l], peer_keys[l], peer_u[l], peer_v[l])
        ks.append(k)
        vs.append(v)
    y_prompt = rmsnorm(x, final_g)
    new_k = jnp.stack(ks, axis=1)
    new_v = jnp.stack(vs, axis=1)

    x = x_sample
    for l in range(DEPTH):
        kind, j = l % N_MIXERS, l // N_MIXERS
        mod = (jax.nn.silu(c) @ w_ada[l] + b_ada[l])[:, None, :]
        sh1, sc1, g1, sh2, sc2, g2 = jnp.split(mod, 6, axis=-1)
        q, k, v = project_qkv(modulate(x, norm_g[l, 0], sh1, sc1), w_qkv[l])
        ck = cache_k[:, l]
        cv = cache_v[:, l]
        if kind == MIX_NA:
            o = neighbourhood_attention(group_heads(q), k, v, ck, cv, na_rpb[j])
        elif kind == MIX_SWA:
            o = sliding_window_attention(group_heads(rope_axial(q)), rope_axial(k), v, ck, cv, swa_sink[j])
        else:
            q = rope_axial(rmsnorm(q, qk_gain[j, 0]))
            k = rope_axial(rmsnorm(k, qk_gain[j, 1]))
            o = dense_block_attention(group_heads(q), k, v, ck, cv)
        x = x + g1 * (o @ w_o[l])
        x = x + g2 * peer_ffn(modulate(x, norm_g[l, 1], sh2, sc2), peer_wq[l], peer_keys[l], peer_u[l], peer_v[l])
    y_sample = rmsnorm(x, final_g)
    return (y_prompt, y_sample, new_k, new_v)
```

```python
import functools

import numpy as np
import jax
import jax.numpy as jnp
from jax import lax
from jax.experimental import pallas as pl
from jax.experimental.pallas import tpu as pltpu

f32 = jnp.float32
bf16 = jnp.bfloat16

D_MODEL = 1024
BATCH = 32
SEQ = 256
DEPTH = 4
DEC_BATCH = 2
DEC_SEQ = 4096
PAST_LEN = 256
GRID_W = 64
N_HEADS = 16
N_KV_HEADS = 4
HEAD_DIM = 64
GQ = N_HEADS // N_KV_HEADS
KV_DIM = N_KV_HEADS * HEAD_DIM
ATTN_SCALE = HEAD_DIM ** -0.5
BLOCK = 128
SWA_WINDOW = 128
NA_KH = 8
NA_KW = 16
ROPE_THETA = 10000.0
NORM_EPS = 1e-6
NEG_INF = -1e30
PEER_HEADS = 8
PEER_N_KEYS = 128
PEER_TOPK = 16
PEER_DK = 128

T_CTX = BATCH * SEQ
T_DEC = DEC_BATCH * DEC_SEQ
T_ALL = T_CTX + T_DEC
N_MOD_ROWS = 8
N_PAIRS = PEER_HEADS * PEER_TOPK
HALF_EXPERTS = PEER_N_KEYS * PEER_N_KEYS // 2
SUBLANES = 8
LANES = 128
TB = 128
TM_QKV = 512
TM_POST = 256
VMEM_LIMIT_EXPERT = 56 * 1024 * 1024
VMEM_LIMIT_ATTN = 48 * 1024 * 1024


def _dot(a, b):
    return jnp.dot(a, b, preferred_element_type=f32)


def _dot_nt(a, b):
    return lax.dot_general(a, b, (((1,), (1,)), ((), ())), preferred_element_type=f32)


def _split(a):
    hi = a.astype(bf16)
    return hi, (a - hi.astype(f32)).astype(bf16)


def _rmsnorm(x, g):
    return x * lax.rsqrt(jnp.mean(x * x, axis=-1, keepdims=True) + NORM_EPS) * g


def _mod_row_map(rows_per_block):
    ctx_blocks = T_CTX // rows_per_block
    per_batch = DEC_SEQ // rows_per_block
    return lambda i: (jnp.where(i < ctx_blocks, 0, 1 + (i - ctx_blocks) // per_batch), 0, 0)


def _ada_kernel(c_ref, w_ref, b_ref, o_ref):
    c = c_ref[...]
    s = c * (1.0 / (1.0 + jnp.exp(-c)))
    sh, sl = _split(s)
    wh, wl = _split(w_ref[0])
    o_ref[0] = _dot(sh, wh) + _dot(sh, wl) + _dot(sl, wh) + b_ref[0]


def ada_mods(cvec, w_ada, b_ada):
    n_out = 6 * D_MODEL
    tn = 1536
    return pl.pallas_call(
        _ada_kernel,
        grid=(DEPTH, n_out // tn),
        in_specs=[
            pl.BlockSpec((N_MOD_ROWS, D_MODEL), lambda l, j: (0, 0)),
            pl.BlockSpec((1, D_MODEL, tn), lambda l, j: (l, 0, j)),
            pl.BlockSpec((1, 1, tn), lambda l, j: (l, 0, j)),
        ],
        out_specs=pl.BlockSpec((1, N_MOD_ROWS, tn), lambda l, j: (l, 0, j)),
        out_shape=jax.ShapeDtypeStruct((DEPTH, N_MOD_ROWS, n_out), f32),
        compiler_params=pltpu.CompilerParams(dimension_semantics=("arbitrary", "arbitrary")),
        name="ada_mods",
    )(cvec, w_ada, b_ada.reshape(DEPTH, 1, n_out))


def _head_norm(x, bd, gain):
    hi, lo = _split(x * x)
    ms = (_dot(hi, bd) + _dot(lo, bd)) * (1.0 / HEAD_DIM)
    return x * lax.rsqrt(ms + NORM_EPS) * gain


def _rope(x, cos, sin_up, sin_dn):
    return (x * cos + pltpu.roll(x, LANES - 16, 1) * sin_up + pltpu.roll(x, 16, 1) * sin_dn)


def _qkv_kernel(*refs, norm_qk, rope):
    refs = list(refs)
    x_ref, mod_ref, g_ref, w_ref = refs[:4]
    refs = refs[4:]
    if norm_qk:
        bd_ref, gq_ref, gk_ref = refs[:3]
        refs = refs[3:]
    if rope:
        cos_ref, sup_ref, sdn_ref = refs[:3]
        refs = refs[3:]
    q_ref, k_ref, v_ref, kb_ref, vb_ref = refs

    mod = mod_ref[...]
    sh1 = mod[:, 0:D_MODEL]
    sc1 = mod[:, D_MODEL:2 * D_MODEL]
    h = _rmsnorm(x_ref[...], g_ref[...]) * (1.0 + sc1) + sh1
    qkv = _dot(h.astype(bf16), w_ref[...])
    q = qkv[:, :D_MODEL]
    k = qkv[:, D_MODEL:D_MODEL + KV_DIM]
    v = qkv[:, D_MODEL + KV_DIM:]
    if norm_qk:
        q = _head_norm(q, bd_ref[...], gq_ref[...])
        k = _head_norm(k, bd_ref[:KV_DIM, :KV_DIM], gk_ref[...])
    k_ref[...] = k
    v_ref[...] = v
    vb_ref[...] = v.astype(bf16)
    if rope:
        is_dec = pl.program_id(0) >= T_CTX // TM_QKV
        cos = jnp.where(is_dec, cos_ref[...], 1.0)
        sup = jnp.where(is_dec, sup_ref[...], 0.0)
        sdn = jnp.where(is_dec, sdn_ref[...], 0.0)
        for j in range(D_MODEL // LANES):
            sl = slice(j * LANES, (j + 1) * LANES)
            q_ref[:, sl] = _rope(q[:, sl], cos, sup, sdn).astype(bf16)
        for j in range(KV_DIM // LANES):
            sl = slice(j * LANES, (j + 1) * LANES)
            kb_ref[:, sl] = _rope(k[:, sl], cos, sup, sdn).astype(bf16)
    else:
        q_ref[...] = q.astype(bf16)
        kb_ref[...] = k.astype(bf16)


def qkv_project(x, mod3, g, w, norm_args, rope_args):
    tm = TM_QKV
    n_blocks = T_ALL // tm
    ctx_blocks = T_CTX // tm
    per_batch = DEC_SEQ // tm
    row = lambda i: (i, 0)
    full = lambda i: (0, 0)
    in_specs = [
        pl.BlockSpec((tm, D_MODEL), row),
        pl.BlockSpec((None, 1, 6 * D_MODEL), _mod_row_map(tm)),
        pl.BlockSpec((1, D_MODEL), full),
        pl.BlockSpec((D_MODEL, D_MODEL + 2 * KV_DIM), full),
    ]
    args = [x, mod3, g, w]
    if norm_args is not None:
        in_specs += [pl.BlockSpec((D_MODEL, D_MODEL), full),
                     pl.BlockSpec((1, D_MODEL), full), pl.BlockSpec((1, KV_DIM), full)]
        args += list(norm_args)
    if rope_args is not None:
        pos_map = lambda i: (jnp.maximum(i - ctx_blocks, 0) % per_batch, 0)
        in_specs += [pl.BlockSpec((tm, LANES), pos_map)] * 3
        args += list(rope_args)
    return pl.pallas_call(
        functools.partial(_qkv_kernel, norm_qk=norm_args is not None, rope=rope_args is not None),
        grid=(n_blocks,),
        in_specs=in_specs,
        out_specs=[
            pl.BlockSpec((tm, D_MODEL), row),
            pl.BlockSpec((tm, KV_DIM), row),
            pl.BlockSpec((tm, KV_DIM), row),
            pl.BlockSpec((tm, KV_DIM), row),
            pl.BlockSpec((tm, KV_DIM), row),
        ],
        out_shape=[
            jax.ShapeDtypeStruct((T_ALL, D_MODEL), bf16),
            jax.ShapeDtypeStruct((T_ALL, KV_DIM), f32),
            jax.ShapeDtypeStruct((T_ALL, KV_DIM), f32),
            jax.ShapeDtypeStruct((T_ALL, KV_DIM), bf16),
            jax.ShapeDtypeStruct((T_ALL, KV_DIM), bf16),
        ],
        compiler_params=pltpu.CompilerParams(dimension_semantics=("arbitrary",)),
        name="qkv_project",
    )(*args)


def _stack_heads(q_ref, kv):
    return jnp.concatenate(
        [q_ref[:, (kv * GQ + g) * HEAD_DIM:(kv * GQ + g + 1) * HEAD_DIM] for g in range(GQ)], axis=0)


def _unstack_heads(o, rows):
    return jnp.concatenate([o[g * rows:(g + 1) * rows] for g in range(GQ)], axis=1)


def _sink_column(sink_ref, kv, rows):
    rid = lax.broadcasted_iota(jnp.int32, (GQ * rows, 1), 0)
    col = jnp.full((GQ * rows, 1), sink_ref[kv * GQ + GQ - 1], f32)
    for g in range(GQ - 2, -1, -1):
        col = jnp.where(rid < (g + 1) * rows, sink_ref[kv * GQ + g], col)
    return col


def _softmax_pv(scores, values, sink_col):
    m = functools.reduce(jnp.maximum, [jnp.max(s, axis=-1, keepdims=True) for s in scores])
    if sink_col is not None:
        m = jnp.maximum(m, sink_col)
    ps = [jnp.exp(s - m) for s in scores]
    d = functools.reduce(lambda a, b: a + b, [jnp.sum(p, axis=-1, keepdims=True) for p in ps])
    if sink_col is not None:
        d = d + jnp.exp(sink_col - m)
    o = functools.reduce(lambda a, b: a + b, [_dot(p.astype(bf16), v) for p, v in zip(ps, values)])
    return o / d


def _kv_slice(ref_or_val, kv):
    return ref_or_val[:, kv * HEAD_DIM:(kv + 1) * HEAD_DIM]


def _ctx_attn_kernel(*refs, has_sink):
    if has_sink:
        sink_ref, q_ref, k_ref, v_ref, o_ref = refs
    else:
        q_ref, k_ref, v_ref, o_ref = refs
    for kv in range(N_KV_HEADS):
        qg = _stack_heads(q_ref, kv)
        s = _dot_nt(qg, _kv_slice(k_ref, kv)) * ATTN_SCALE
        sink_col = _sink_column(sink_ref, kv, SEQ) if has_sink else None
        o = _softmax_pv([s], [_kv_slice(v_ref, kv)], sink_col)
        o_ref[:, kv * GQ * HEAD_DIM:(kv + 1) * GQ * HEAD_DIM] = _unstack_heads(o, SEQ).astype(bf16)


def ctx_attention(qb, kb, vb, sink):
    row = lambda b: (b, 0)
    in_specs = [pl.BlockSpec((SEQ, D_MODEL), row), pl.BlockSpec((SEQ, KV_DIM), row),
                pl.BlockSpec((SEQ, KV_DIM), row)]
    args = [qb, kb, vb]
    if sink is not None:
        in_specs = [pl.BlockSpec(memory_space=pltpu.SMEM)] + in_specs
        args = [sink] + args
    return pl.pallas_call(
        functools.partial(_ctx_attn_kernel, has_sink=sink is not None),
        grid=(BATCH,),
        in_specs=in_specs,
        out_specs=pl.BlockSpec((SEQ, D_MODEL), row),
        out_shape=jax.ShapeDtypeStruct((T_CTX, D_MODEL), bf16),
        compiler_params=pltpu.CompilerParams(dimension_semantics=("arbitrary",)),
        name="ctx_attention",
    )(*args)


def _na_attn_kernel(q_ref, k_ref, v_ref, ck_ref, cv_ref, bias_ref, o_ref):
    r = pl.program_id(1)
    rows = DEC_SEQ // GRID_W
    rs = jnp.clip(r - NA_KH // 2, 0, rows - NA_KH)
    dr0 = rs - r + (NA_KH - 1)
    start = pl.multiple_of(rs * GRID_W, GRID_W)
    kwin = k_ref[pl.ds(start, NA_KH * GRID_W), :]
    vwin = v_ref[pl.ds(start, NA_KH * GRID_W), :]
    for kv in range(N_KV_HEADS):
        qg = _stack_heads(q_ref, kv)
        bias = jnp.concatenate(
            [jnp.concatenate([bias_ref[kv * GQ + g, dr0 + 2 * j] for j in range(NA_KH // 2)], axis=1)
             for g in range(GQ)], axis=0)
        s_loc = _dot_nt(qg, _kv_slice(kwin, kv)) * ATTN_SCALE + bias
        s_ctx = _dot_nt(qg, _kv_slice(ck_ref, kv)) * ATTN_SCALE
        o = _softmax_pv([s_loc, s_ctx], [_kv_slice(vwin, kv), _kv_slice(cv_ref, kv)], None)
        o_ref[:, kv * GQ * HEAD_DIM:(kv + 1) * GQ * HEAD_DIM] = _unstack_heads(o, GRID_W).astype(bf16)


def _dec_specs(q_rows):
    q_off = T_CTX // q_rows
    per_batch = DEC_SEQ // q_rows
    kv_off = T_CTX // DEC_SEQ
    return [
        pl.BlockSpec((q_rows, D_MODEL), lambda b, n: (q_off + b * per_batch + n, 0)),
        pl.BlockSpec((DEC_SEQ, KV_DIM), lambda b, n: (kv_off + b, 0)),
        pl.BlockSpec((DEC_SEQ, KV_DIM), lambda b, n: (kv_off + b, 0)),
        pl.BlockSpec((None, PAST_LEN, KV_DIM), lambda b, n: (b, 0, 0)),
        pl.BlockSpec((None, PAST_LEN, KV_DIM), lambda b, n: (b, 0, 0)),
    ], pl.BlockSpec((q_rows, D_MODEL), lambda b, n: (b * per_batch + n, 0))


def na_attention(qb, kb, vb, ck, cv, bias2):
    in_specs, out_spec = _dec_specs(GRID_W)
    return pl.pallas_call(
        _na_attn_kernel,
        grid=(DEC_BATCH, DEC_SEQ // GRID_W),
        in_specs=in_specs + [pl.BlockSpec(memory_space=pltpu.VMEM)],
        out_specs=out_spec,
        out_shape=jax.ShapeDtypeStruct((T_DEC, D_MODEL), bf16),
        compiler_params=pltpu.CompilerParams(
            dimension_semantics=("arbitrary", "arbitrary"), vmem_limit_bytes=VMEM_LIMIT_ATTN),
        name="na_attention",
    )(qb, kb, vb, ck, cv, bias2)


def _swa_attn_kernel(sink_ref, q_ref, k_ref, v_ref, ck_ref, cv_ref, o_ref):
    n = pl.program_id(1)
    band = 3 * BLOCK
    start = pl.multiple_of(jnp.clip((n - 1) * BLOCK, 0, DEC_SEQ - band), BLOCK)
    kband = k_ref[pl.ds(start, band), :]
    vband = v_ref[pl.ds(start, band), :]
    rid = lax.broadcasted_iota(jnp.int32, (GQ * BLOCK, band), 0)
    cid = lax.broadcasted_iota(jnp.int32, (GQ * BLOCK, band), 1)
    q_pos = n * BLOCK + (rid & (BLOCK - 1))
    k_pos = start + cid
    allowed = jnp.abs(q_pos - k_pos) <= SWA_WINDOW
    for kv in range(N_KV_HEADS):
        qg = _stack_heads(q_ref, kv)
        s_loc = jnp.where(allowed, _dot_nt(qg, _kv_slice(kband, kv)) * ATTN_SCALE, NEG_INF)
        s_ctx = _dot_nt(qg, _kv_slice(ck_ref, kv)) * ATTN_SCALE
        o = _softmax_pv([s_loc, s_ctx], [_kv_slice(vband, kv), _kv_slice(cv_ref, kv)],
                        _sink_column(sink_ref, kv, BLOCK))
        o_ref[:, kv * GQ * HEAD_DIM:(kv + 1) * GQ * HEAD_DIM] = _unstack_heads(o, BLOCK).astype(bf16)


def swa_attention(qb, kb, vb, ck, cv, sink):
    in_specs, out_spec = _dec_specs(BLOCK)
    return pl.pallas_call(
        _swa_attn_kernel,
        grid=(DEC_BATCH, DEC_SEQ // BLOCK),
        in_specs=[pl.BlockSpec(memory_space=pltpu.SMEM)] + in_specs,
        out_specs=out_spec,
        out_shape=jax.ShapeDtypeStruct((T_DEC, D_MODEL), bf16),
        compiler_params=pltpu.CompilerParams(
            dimension_semantics=("arbitrary", "arbitrary"), vmem_limit_bytes=VMEM_LIMIT_ATTN),
        name="swa_attention",
    )(sink, qb, kb, vb, ck, cv)


def _dense_attn_kernel(q_ref, k_ref, v_ref, ck_ref, cv_ref, o_ref):
    for kv in range(N_KV_HEADS):
        qg = _stack_heads(q_ref, kv)
        s_lat = _dot_nt(qg, _kv_slice(k_ref, kv)) * ATTN_SCALE
        s_ctx = _dot_nt(qg, _kv_slice(ck_ref, kv)) * ATTN_SCALE
        o = _softmax_pv([s_lat, s_ctx], [_kv_slice(v_ref, kv), _kv_slice(cv_ref, kv)], None)
        o_ref[:, kv * GQ * HEAD_DIM:(kv + 1) * GQ * HEAD_DIM] = _unstack_heads(o, BLOCK).astype(bf16)


def dense_attention(qb, kb, vb, ck, cv):
    in_specs, out_spec = _dec_specs(BLOCK)
    return pl.pallas_call(
        _dense_attn_kernel,
        grid=(DEC_BATCH, DEC_SEQ // BLOCK),
        in_specs=in_specs,
        out_specs=out_spec,
        out_shape=jax.ShapeDtypeStruct((T_DEC, D_MODEL), bf16),
        compiler_params=pltpu.CompilerParams(
            dimension_semantics=("arbitrary", "arbitrary"), vmem_limit_bytes=VMEM_LIMIT_ATTN),
        name="dense_attention",
    )(qb, kb, vb, ck, cv)


def _topk_rows(s, ids, k):
    vals, labels = [], []
    for _ in range(k):
        m = jnp.max(s, axis=0, keepdims=True)
        lab = jnp.min(jnp.where(s == m, ids, 1e9), axis=0, keepdims=True)
        s = jnp.where(ids == lab, -jnp.inf, s)
        vals.append(m)
        labels.append(lab)
    return jnp.concatenate(vals, axis=0), jnp.concatenate(labels, axis=0)


def _pick(labels, table):
    out = jnp.zeros_like(labels)
    for a in range(PEER_TOPK):
        out = out + jnp.where(labels == float(a), table[a:a + 1], 0.0)
    return out


def _post_router_kernel(x_ref, o_ref, mod_ref, wo_ref, g_ref, wq_ref, keys_ref,
                        x1_ref, h2_ref, r8_ref, sh_ref, gate_ref, qp_scr):
    mod = mod_ref[...]
    g1 = mod[:, 2 * D_MODEL:3 * D_MODEL]
    sh2 = mod[:, 3 * D_MODEL:4 * D_MODEL]
    sc2 = mod[:, 4 * D_MODEL:5 * D_MODEL]
    x1 = x_ref[...] + g1 * _dot(o_ref[...], wo_ref[...])
    x1_ref[...] = x1
    h2 = _rmsnorm(x1, g_ref[...]) * (1.0 + sc2) + sh2
    h2_ref[...] = h2
    qp_scr[...] = _dot(h2.astype(bf16), wq_ref[...])

    key_ids = lax.broadcasted_iota(jnp.int32, (PEER_N_KEYS, LANES), 0).astype(f32)
    b_ids = lax.broadcasted_iota(jnp.int32, (PEER_TOPK, LANES), 0)
    cand_ids = jnp.concatenate(
        [(b_ids + a * PEER_TOPK).astype(f32) for a in range(8)]
        + [((b_ids[:8] + 8) * PEER_TOPK).astype(f32)], axis=0)
    n_chunks = TM_POST // LANES

    def head_chunk(hc, carry):
        h = hc // n_chunks
        c = hc % n_chunks
        rows = pl.ds(pl.multiple_of(c * LANES, LANES), LANES)
        tops = []
        for p in range(2):
            cols = pl.ds(pl.multiple_of((h * 2 + p) * PEER_DK, PEER_DK), PEER_DK)
            s = _dot_nt(keys_ref[h * 2 + p], qp_scr[rows, cols].astype(bf16))
            tops.append(_topk_rows(s, key_ids, PEER_TOPK))
        (s1, i1), (s2, i2) = tops
        pieces = [jnp.where(b_ids < PEER_TOPK // (a + 1), s1[a:a + 1] + s2, -jnp.inf) for a in range(8)]
        pieces.append(s1[8:] + s2[0:1])
        top_s, pos = _topk_rows(jnp.concatenate(pieces, axis=0), cand_ids, PEER_TOPK)
        a_lab = jnp.floor(pos * (1.0 / PEER_TOPK))
        b_lab = pos - a_lab * PEER_TOPK
        idx = (_pick(a_lab, i1) * PEER_N_KEYS + _pick(b_lab, i2)).astype(jnp.int32)
        e = jnp.exp(top_s - top_s[0:1])
        gate = e / jnp.sum(e, axis=0, keepdims=True)
        out_rows = pl.ds(pl.multiple_of(h * PEER_TOPK, PEER_TOPK), PEER_TOPK)
        r8_ref[out_rows, rows] = (idx & (HALF_EXPERTS - 1)) * SUBLANES
        sh_ref[out_rows, rows] = 16 - ((idx >> 13) << 4)
        gate_ref[out_rows, rows] = gate
        return carry

    lax.fori_loop(0, PEER_HEADS * n_chunks, head_chunk, 0)


def post_router(x, o, mod3, wo, g, wq, keys):
    tm = TM_POST
    row = lambda i: (i, 0)
    col = lambda i: (0, i)
    full = lambda i: (0, 0)
    return pl.pallas_call(
        _post_router_kernel,
        grid=(T_ALL // tm,),
        in_specs=[
            pl.BlockSpec((tm, D_MODEL), row),
            pl.BlockSpec((tm, D_MODEL), row),
            pl.BlockSpec((None, 1, 6 * D_MODEL), _mod_row_map(tm)),
            pl.BlockSpec((D_MODEL, D_MODEL), full),
            pl.BlockSpec((1, D_MODEL), full),
            pl.BlockSpec((D_MODEL, 2 * PEER_HEADS * PEER_DK), full),
            pl.BlockSpec((2 * PEER_HEADS, PEER_N_KEYS, PEER_DK), lambda i: (0, 0, 0)),
        ],
        out_specs=[
            pl.BlockSpec((tm, D_MODEL), row),
            pl.BlockSpec((tm, D_MODEL), row),
            pl.BlockSpec((N_PAIRS, tm), col),
            pl.BlockSpec((N_PAIRS, tm), col),
            pl.BlockSpec((N_PAIRS, tm), col),
        ],
        out_shape=[
            jax.ShapeDtypeStruct((T_ALL, D_MODEL), f32),
            jax.ShapeDtypeStruct((T_ALL, D_MODEL), f32),
            jax.ShapeDtypeStruct((N_PAIRS, T_ALL), jnp.int32),
            jax.ShapeDtypeStruct((N_PAIRS, T_ALL), jnp.int32),
            jax.ShapeDtypeStruct((N_PAIRS, T_ALL), f32),
        ],
        scratch_shapes=[pltpu.VMEM((tm, 2 * PEER_HEADS * PEER_DK), f32)],
        compiler_params=pltpu.CompilerParams(dimension_semantics=("arbitrary",)),
        name="post_router",
    )(x, o, mod3, wo, g, wq, keys)


def _unpack_expert(tab_ref, r8, sh):
    word = tab_ref[pl.ds(pl.multiple_of(r8, SUBLANES), SUBLANES), :]
    return pltpu.bitcast((word << sh.astype(jnp.uint32)) & jnp.uint32(0xFFFF0000), f32)


def _sublane_sums(ps):
    sub = lax.broadcasted_iota(jnp.int32, (SUBLANES, LANES), 0)
    m4 = sub < 4
    m2 = (sub & 2) == 0
    m1 = (sub & 1) == 0
    lvl1 = [jnp.where(m4, ps[a], ps[b]) + pltpu.roll(jnp.where(m4, ps[b], ps[a]), 4, 0)
            for a, b in ((0, 4), (2, 6), (1, 5), (3, 7))]
    lvl2 = [jnp.where(m2, x, pltpu.roll(y, 2, 0)) + jnp.where(m2, pltpu.roll(x, 6, 0), y)
            for x, y in ((lvl1[0], lvl1[1]), (lvl1[2], lvl1[3]))]
    x, y = lvl2
    return jnp.where(m1, x, pltpu.roll(y, 1, 0)) + jnp.where(m1, pltpu.roll(x, 7, 0), y)


def _expert_u_kernel(r8_ref, sh_ref, x_ref, gate_ref, tab_ref, fsel_ref, w_ref, part_scr):
    def token(t, carry):
        xt = x_ref[t]
        lanes = pl.ds(pl.multiple_of(t * LANES, LANES), LANES)
        for j in range(N_PAIRS // SUBLANES):
            ps = [_unpack_expert(tab_ref, r8_ref[t, j * SUBLANES + s], sh_ref[t, j * SUBLANES + s]) * xt
                  for s in range(SUBLANES)]
            part_scr[j * SUBLANES:(j + 1) * SUBLANES, lanes] = _sublane_sums(ps)
        return carry

    lax.fori_loop(0, TB, token, 0)
    hi, lo = _split(part_scr[...])
    fsel = fsel_ref[...]
    a = _dot(hi, fsel) + _dot(lo, fsel)
    gelu = 0.5 * a * (1.0 + lax.erf(a * np.float32(2.0 ** -0.5)))
    w_ref[...] = gelu * gate_ref[...]


def expert_u(r8_t, sh_t, x3, gate, tab, fsel):
    tok = lambda i: (i, 0)
    col = lambda i: (0, i)
    return pl.pallas_call(
        _expert_u_kernel,
        grid=(T_ALL // TB,),
        in_specs=[
            pl.BlockSpec((TB, N_PAIRS), tok, memory_space=pltpu.SMEM),
            pl.BlockSpec((TB, N_PAIRS), tok, memory_space=pltpu.SMEM),
            pl.BlockSpec((TB, SUBLANES, LANES), lambda i: (i, 0, 0)),
            pl.BlockSpec((N_PAIRS, TB), col),
            pl.BlockSpec(memory_space=pltpu.VMEM),
            pl.BlockSpec(memory_space=pltpu.VMEM),
        ],
        out_specs=pl.BlockSpec((N_PAIRS, TB), col),
        out_shape=jax.ShapeDtypeStruct((N_PAIRS, T_ALL), f32),
        scratch_shapes=[pltpu.VMEM((N_PAIRS, TB * LANES), f32)],
        compiler_params=pltpu.CompilerParams(
            dimension_semantics=("arbitrary",), vmem_limit_bytes=VMEM_LIMIT_EXPERT),
        name="expert_u",
    )(r8_t, sh_t, x3, gate, tab, fsel)


def _expert_v_kernel(r8_ref, sh_ref, w_ref, x1_ref, g2_ref, tab_ref, o_ref, wb_scr):
    w = w_ref[...]
    for t in range(TB):
        wb_scr[t] = jnp.broadcast_to(w[:, t:t + 1], (N_PAIRS, LANES))
    g2 = g2_ref[0]

    def token(t, carry):
        accs = [jnp.zeros((SUBLANES, LANES), f32) for _ in range(4)]
        for k in range(N_PAIRS):
            val = _unpack_expert(tab_ref, r8_ref[t, k], sh_ref[t, k])
            wk = jnp.broadcast_to(wb_scr[t, k:k + 1, :], (SUBLANES, LANES))
            accs[k % 4] = accs[k % 4] + wk * val
        o_ref[t] = x1_ref[t] + g2 * ((accs[0] + accs[1]) + (accs[2] + accs[3]))
        return carry

    lax.fori_loop(0, TB, token, 0)


def expert_v(r8_t, sh_t, w, x1_3, g2_3, tab):
    tok = lambda i: (i, 0)
    tok3 = lambda i: (i, 0, 0)
    return pl.pallas_call(
        _expert_v_kernel,
        grid=(T_ALL // TB,),
        in_specs=[
            pl.BlockSpec((TB, N_PAIRS), tok, memory_space=pltpu.SMEM),
            pl.BlockSpec((TB, N_PAIRS), tok, memory_space=pltpu.SMEM),
            pl.BlockSpec((N_PAIRS, TB), lambda i: (0, i)),
            pl.BlockSpec((TB, SUBLANES, LANES), tok3),
            pl.BlockSpec((1, SUBLANES, LANES), _mod_row_map(TB)),
            pl.BlockSpec(memory_space=pltpu.VMEM),
        ],
        out_specs=pl.BlockSpec((TB, SUBLANES, LANES), tok3),
        out_shape=jax.ShapeDtypeStruct((T_ALL, SUBLANES, LANES), f32),
        scratch_shapes=[pltpu.VMEM((TB, N_PAIRS, LANES), f32)],
        compiler_params=pltpu.CompilerParams(
            dimension_semantics=("arbitrary",), vmem_limit_bytes=VMEM_LIMIT_EXPERT),
        name="expert_v",
    )(r8_t, sh_t, w, x1_3, g2_3, tab)


def _final_norm_kernel(x_ref, g_ref, o_ref):
    o_ref[...] = _rmsnorm(x_ref[...], g_ref[...])


def final_norm(x, g):
    tm = TM_QKV
    return pl.pallas_call(
        _final_norm_kernel,
        grid=(T_ALL // tm,),
        in_specs=[pl.BlockSpec((tm, D_MODEL), lambda i: (i, 0)), pl.BlockSpec((1, D_MODEL), lambda i: (0, 0))],
        out_specs=pl.BlockSpec((tm, D_MODEL), lambda i: (i, 0)),
        out_shape=jax.ShapeDtypeStruct((T_ALL, D_MODEL), f32),
        compiler_params=pltpu.CompilerParams(dimension_semantics=("arbitrary",)),
        name="final_norm",
    )(x, g)


def _rope_tables():
    pos = np.arange(DEC_SEQ)
    d = np.arange(LANES) % HEAD_DIM
    quarter = HEAD_DIM // 4
    freq = ROPE_THETA ** (-(d % quarter).astype(np.float64) / quarter)
    coord = np.where((d < HEAD_DIM // 2)[None, :], (pos // GRID_W)[:, None], (pos % GRID_W)[:, None])
    ang = coord.astype(np.float64) * freq[None, :]
    first = ((d % (HEAD_DIM // 2)) < quarter)[None, :]
    cos = np.cos(ang)
    sin = np.sin(ang)
    return (jnp.asarray(cos, f32), jnp.asarray(np.where(first, -sin, 0.0), f32),
            jnp.asarray(np.where(first, 0.0, sin), f32))


def _na_bias_pairs(rpb):
    col = np.arange(GRID_W)
    col_start = np.clip(col - NA_KW // 2, 0, GRID_W - NA_KW)
    c2 = np.arange(GRID_W)
    inside = (c2[None, :] >= col_start[:, None]) & (c2[None, :] < col_start[:, None] + NA_KW)
    dc = np.clip(c2[None, :] - col[:, None] + (NA_KW - 1), 0, 2 * NA_KW - 2)
    full = jnp.where(jnp.asarray(inside)[None, None], rpb[:, :, dc], NEG_INF)
    return jnp.concatenate([full[:, :-1], full[:, 1:]], axis=-1)


def _pack_table(t):
    bits = lax.bitcast_convert_type(t.astype(bf16), jnp.uint16).astype(jnp.uint32)
    packed = bits[:HALF_EXPERTS] | (bits[HALF_EXPERTS:] << 16)
    return packed.reshape(HALF_EXPERTS * SUBLANES, LANES)


def kernel(x_prompt, x_sample, cache_k, cache_v, c, c_ctx, w_ada, b_ada, norm_g, final_g, w_qkv, w_o, na_rpb, swa_sink, qk_gain, peer_wq, peer_keys, peer_u, peer_v):
    cvec = jnp.concatenate([c_ctx[None], c, jnp.zeros((N_MOD_ROWS - 1 - DEC_BATCH, D_MODEL), f32)], axis=0)
    mods = ada_mods(cvec, w_ada, b_ada)
    x = jnp.concatenate([x_prompt.reshape(T_CTX, D_MODEL), x_sample.reshape(T_DEC, D_MODEL)], axis=0)

    rope_tabs = _rope_tables()
    head_sum = jnp.asarray(np.kron(np.eye(N_HEADS), np.ones((HEAD_DIM, HEAD_DIM))), bf16)
    fsel = jnp.asarray(np.kron(np.eye(TB), np.ones((LANES, 1))), bf16)

    ks, vs = [], []
    for l in range(DEPTH):
        kind, j = l % 3, l // 3
        mod3 = mods[l].reshape(N_MOD_ROWS, 1, 6 * D_MODEL)
        norm_args = None
        if kind == 2:
            norm_args = (head_sum, jnp.tile(qk_gain[j, 0], N_HEADS)[None], jnp.tile(qk_gain[j, 1], N_KV_HEADS)[None])
        rope_args = rope_tabs if kind != 0 else None
        qb, k, v, kb, vb = qkv_project(x, mod3, norm_g[l, 0][None], w_qkv[l].astype(bf16), norm_args, rope_args)
        ks.append(k[:T_CTX].reshape(BATCH, SEQ, N_KV_HEADS, HEAD_DIM))
        vs.append(v[:T_CTX].reshape(BATCH, SEQ, N_KV_HEADS, HEAD_DIM))

        ck = cache_k[:, l].reshape(DEC_BATCH, PAST_LEN, KV_DIM).astype(bf16)
        cv = cache_v[:, l].reshape(DEC_BATCH, PAST_LEN, KV_DIM).astype(bf16)
        o_ctx = ctx_attention(qb, kb, vb, swa_sink[j] if kind == 1 else None)
        if kind == 0:
            o_dec = na_attention(qb, kb, vb, ck, cv, _na_bias_pairs(na_rpb[j]))
        elif kind == 1:
            o_dec = swa_attention(qb, kb, vb, ck, cv, swa_sink[j])
        else:
            o_dec = dense_attention(qb, kb, vb, ck, cv)
        o = jnp.concatenate([o_ctx, o_dec], axis=0)

        x1, h2, r8, sh, gate = post_router(
            x, o, mod3, w_o[l].astype(bf16), norm_g[l, 1][None], peer_wq[l].astype(bf16),
            peer_keys[l].reshape(2 * PEER_HEADS, PEER_N_KEYS, PEER_DK).astype(bf16))
        r8_t, sh_t = r8.T, sh.T
        w = expert_u(r8_t, sh_t, h2.reshape(T_ALL, SUBLANES, LANES), gate, _pack_table(peer_u[l]), fsel)
        g2 = mods[l][:, 5 * D_MODEL:].reshape(N_MOD_ROWS, SUBLANES, LANES)
        x = expert_v(r8_t, sh_t, w, x1.reshape(T_ALL, SUBLANES, LANES), g2,
                     _pack_table(peer_v[l])).reshape(T_ALL, D_MODEL)

    y = final_norm(x, final_g[None])
    y_prompt = y[:T_CTX].reshape(BATCH, SEQ, D_MODEL)
    y_sample = y[T_CTX:].reshape(DEC_BATCH, DEC_SEQ, D_MODEL)
    return (y_prompt, y_sample, jnp.stack(ks, axis=1), jnp.stack(vs, axis=1))
```

```python
import functools

import numpy as np
import jax
import jax.numpy as jnp
from jax import lax
from jax.experimental import pallas as pl
from jax.experimental.pallas import tpu as pltpu

f32 = jnp.float32
bf16 = jnp.bfloat16

D_MODEL = 1024
BATCH = 32
SEQ = 256
DEPTH = 4
DEC_BATCH = 2
DEC_SEQ = 4096
PAST_LEN = 256
GRID_W = 64
N_HEADS = 16
N_KV_HEADS = 4
HEAD_DIM = 64
GQ = N_HEADS // N_KV_HEADS
KV_DIM = N_KV_HEADS * HEAD_DIM
ATTN_SCALE = HEAD_DIM ** -0.5
BLOCK = 128
SWA_WINDOW = 128
NA_KH = 8
NA_KW = 16
ROPE_THETA = 10000.0
NORM_EPS = 1e-6
NEG_INF = -1e30
PEER_HEADS = 8
PEER_N_KEYS = 128
PEER_TOPK = 16
PEER_DK = 128

T_CTX = BATCH * SEQ
T_DEC = DEC_BATCH * DEC_SEQ
T_ALL = T_CTX + T_DEC
N_MOD_ROWS = 8
N_PAIRS = PEER_HEADS * PEER_TOPK
HALF_EXPERTS = PEER_N_KEYS * PEER_N_KEYS // 2
SUBLANES = 8
LANES = 128
TB = 128
TM_QKV = 512
TM_POST = 256
VMEM_LIMIT_EXPERT = 56 * 1024 * 1024
VMEM_LIMIT_ATTN = 48 * 1024 * 1024


def _dot(a, b):
    return jnp.dot(a, b, preferred_element_type=f32)


def _dot_nt(a, b):
    return lax.dot_general(a, b, (((1,), (1,)), ((), ())), preferred_element_type=f32)


def _split(a):
    hi = a.astype(bf16)
    return hi, (a - hi.astype(f32)).astype(bf16)


def _rmsnorm(x, g):
    return x * lax.rsqrt(jnp.mean(x * x, axis=-1, keepdims=True) + NORM_EPS) * g


def _mod_row_map(rows_per_block):
    ctx_blocks = T_CTX // rows_per_block
    per_batch = DEC_SEQ // rows_per_block
    return lambda i: (jnp.where(i < ctx_blocks, 0, 1 + (i - ctx_blocks) // per_batch), 0, 0)


def _ada_kernel(c_ref, w_ref, b_ref, o_ref):
    c = c_ref[...]
    s = c * (1.0 / (1.0 + jnp.exp(-c)))
    sh, sl = _split(s)
    wh, wl = _split(w_ref[0])
    o_ref[0] = _dot(sh, wh) + _dot(sh, wl) + _dot(sl, wh) + b_ref[0]


def ada_mods(cvec, w_ada, b_ada):
    n_out = 6 * D_MODEL
    tn = 1536
    return pl.pallas_call(
        _ada_kernel,
        grid=(DEPTH, n_out // tn),
        in_specs=[
            pl.BlockSpec((N_MOD_ROWS, D_MODEL), lambda l, j: (0, 0)),
            pl.BlockSpec((1, D_MODEL, tn), lambda l, j: (l, 0, j)),
            pl.BlockSpec((1, 1, tn), lambda l, j: (l, 0, j)),
        ],
        out_specs=pl.BlockSpec((1, N_MOD_ROWS, tn), lambda l, j: (l, 0, j)),
        out_shape=jax.ShapeDtypeStruct((DEPTH, N_MOD_ROWS, n_out), f32),
        compiler_params=pltpu.CompilerParams(dimension_semantics=("arbitrary", "arbitrary")),
        name="ada_mods",
    )(cvec, w_ada, b_ada.reshape(DEPTH, 1, n_out))


def _head_norm(x, bd, gain):
    hi, lo = _split(x * x)
    ms = (_dot(hi, bd) + _dot(lo, bd)) * (1.0 / HEAD_DIM)
    return x * lax.rsqrt(ms + NORM_EPS) * gain


def _rope(x, cos, sin_up, sin_dn):
    return (x * cos + pltpu.roll(x, LANES - 16, 1) * sin_up + pltpu.roll(x, 16, 1) * sin_dn)


def _qkv_kernel(*refs, norm_qk, rope):
    refs = list(refs)
    x_ref, mod_ref, g_ref, w_ref = refs[:4]
    refs = refs[4:]
    if norm_qk:
        bd_ref, gq_ref, gk_ref = refs[:3]
        refs = refs[3:]
    if rope:
        cos_ref, sup_ref, sdn_ref = refs[:3]
        refs = refs[3:]
    q_ref, k_ref, v_ref, kb_ref, vb_ref = refs

    mod = mod_ref[...]
    sh1 = mod[:, 0:D_MODEL]
    sc1 = mod[:, D_MODEL:2 * D_MODEL]
    h = _rmsnorm(x_ref[...], g_ref[...]) * (1.0 + sc1) + sh1
    qkv = _dot(h.astype(bf16), w_ref[...])
    q = qkv[:, :D_MODEL]
    k = qkv[:, D_MODEL:D_MODEL + KV_DIM]
    v = qkv[:, D_MODEL + KV_DIM:]
    if norm_qk:
        q = _head_norm(q, bd_ref[...], gq_ref[...])
        k = _head_norm(k, bd_ref[:KV_DIM, :KV_DIM], gk_ref[...])
    k_ref[...] = k
    v_ref[...] = v
    vb_ref[...] = v.astype(bf16)
    if rope:
        is_dec = pl.program_id(0) >= T_CTX // TM_QKV
        cos = jnp.where(is_dec, cos_ref[...], 1.0)
        sup = jnp.where(is_dec, sup_ref[...], 0.0)
        sdn = jnp.where(is_dec, sdn_ref[...], 0.0)
        for j in range(D_MODEL // LANES):
            sl = slice(j * LANES, (j + 1) * LANES)
            q_ref[:, sl] = _rope(q[:, sl], cos, sup, sdn).astype(bf16)
        for j in range(KV_DIM // LANES):
            sl = slice(j * LANES, (j + 1) * LANES)
            kb_ref[:, sl] = _rope(k[:, sl], cos, sup, sdn).astype(bf16)
    else:
        q_ref[...] = q.astype(bf16)
        kb_ref[...] = k.astype(bf16)


def qkv_project(x, mod3, g, w, norm_args, rope_args):
    tm = TM_QKV
    n_blocks = T_ALL // tm
    ctx_blocks = T_CTX // tm
    per_batch = DEC_SEQ // tm
    row = lambda i: (i, 0)
    full = lambda i: (0, 0)
    in_specs = [
        pl.BlockSpec((tm, D_MODEL), row),
        pl.BlockSpec((None, 1, 6 * D_MODEL), _mod_row_map(tm)),
        pl.BlockSpec((1, D_MODEL), full),
        pl.BlockSpec((D_MODEL, D_MODEL + 2 * KV_DIM), full),
    ]
    args = [x, mod3, g, w]
    if norm_args is not None:
        in_specs += [pl.BlockSpec((D_MODEL, D_MODEL), full),
                     pl.BlockSpec((1, D_MODEL), full), pl.BlockSpec((1, KV_DIM), full)]
        args += list(norm_args)
    if rope_args is not None:
        pos_map = lambda i: (jnp.maximum(i - ctx_blocks, 0) % per_batch, 0)
        in_specs += [pl.BlockSpec((tm, LANES), pos_map)] * 3
        args += list(rope_args)
    return pl.pallas_call(
        functools.partial(_qkv_kernel, norm_qk=norm_args is not None, rope=rope_args is not None),
        grid=(n_blocks,),
        in_specs=in_specs,
        out_specs=[
            pl.BlockSpec((tm, D_MODEL), row),
            pl.BlockSpec((tm, KV_DIM), row),
            pl.BlockSpec((tm, KV_DIM), row),
            pl.BlockSpec((tm, KV_DIM), row),
            pl.BlockSpec((tm, KV_DIM), row),
        ],
        out_shape=[
            jax.ShapeDtypeStruct((T_ALL, D_MODEL), bf16),
            jax.ShapeDtypeStruct((T_ALL, KV_DIM), f32),
            jax.ShapeDtypeStruct((T_ALL, KV_DIM), f32),
            jax.ShapeDtypeStruct((T_ALL, KV_DIM), bf16),
            jax.ShapeDtypeStruct((T_ALL, KV_DIM), bf16),
        ],
        compiler_params=pltpu.CompilerParams(dimension_semantics=("arbitrary",)),
        name="qkv_project",
    )(*args)


def _stack_heads(q_ref, kv):
    return jnp.concatenate(
        [q_ref[:, (kv * GQ + g) * HEAD_DIM:(kv * GQ + g + 1) * HEAD_DIM] for g in range(GQ)], axis=0)


def _unstack_heads(o, rows):
    return jnp.concatenate([o[g * rows:(g + 1) * rows] for g in range(GQ)], axis=1)


def _sink_column(sink_ref, kv, rows):
    rid = lax.broadcasted_iota(jnp.int32, (GQ * rows, 1), 0)
    col = jnp.full((GQ * rows, 1), sink_ref[kv * GQ + GQ - 1], f32)
    for g in range(GQ - 2, -1, -1):
        col = jnp.where(rid < (g + 1) * rows, sink_ref[kv * GQ + g], col)
    return col


def _softmax_pv(scores, values, sink_col):
    m = functools.reduce(jnp.maximum, [jnp.max(s, axis=-1, keepdims=True) for s in scores])
    if sink_col is not None:
        m = jnp.maximum(m, sink_col)
    ps = [jnp.exp(s - m) for s in scores]
    d = functools.reduce(lambda a, b: a + b, [jnp.sum(p, axis=-1, keepdims=True) for p in ps])
    if sink_col is not None:
        d = d + jnp.exp(sink_col - m)
    o = functools.reduce(lambda a, b: a + b, [_dot(p.astype(bf16), v) for p, v in zip(ps, values)])
    return o / d


def _kv_slice(ref_or_val, kv):
    return ref_or_val[:, kv * HEAD_DIM:(kv + 1) * HEAD_DIM]


def _ctx_attn_kernel(*refs, has_sink):
    if has_sink:
        sink_ref, q_ref, k_ref, v_ref, o_ref = refs
    else:
        q_ref, k_ref, v_ref, o_ref = refs
    for kv in range(N_KV_HEADS):
        qg = _stack_heads(q_ref, kv)
        s = _dot_nt(qg, _kv_slice(k_ref, kv)) * ATTN_SCALE
        sink_col = _sink_column(sink_ref, kv, SEQ) if has_sink else None
        o = _softmax_pv([s], [_kv_slice(v_ref, kv)], sink_col)
        o_ref[:, kv * GQ * HEAD_DIM:(kv + 1) * GQ * HEAD_DIM] = _unstack_heads(o, SEQ).astype(bf16)


def ctx_attention(qb, kb, vb, sink):
    row = lambda b: (b, 0)
    in_specs = [pl.BlockSpec((SEQ, D_MODEL), row), pl.BlockSpec((SEQ, KV_DIM), row),
                pl.BlockSpec((SEQ, KV_DIM), row)]
    args = [qb, kb, vb]
    if sink is not None:
        in_specs = [pl.BlockSpec(memory_space=pltpu.SMEM)] + in_specs
        args = [sink] + args
    return pl.pallas_call(
        functools.partial(_ctx_attn_kernel, has_sink=sink is not None),
        grid=(BATCH,),
        in_specs=in_specs,
        out_specs=pl.BlockSpec((SEQ, D_MODEL), row),
        out_shape=jax.ShapeDtypeStruct((T_CTX, D_MODEL), bf16),
        compiler_params=pltpu.CompilerParams(dimension_semantics=("arbitrary",)),
        name="ctx_attention",
    )(*args)


def _na_attn_kernel(q_ref, k_ref, v_ref, ck_ref, cv_ref, bias_ref, o_ref):
    r = pl.program_id(1)
    rows = DEC_SEQ // GRID_W
    rs = jnp.clip(r - NA_KH // 2, 0, rows - NA_KH)
    dr0 = rs - r + (NA_KH - 1)
    start = pl.multiple_of(rs * GRID_W, GRID_W)
    kwin = k_ref[pl.ds(start, NA_KH * GRID_W), :]
    vwin = v_ref[pl.ds(start, NA_KH * GRID_W), :]
    for kv in range(N_KV_HEADS):
        qg = _stack_heads(q_ref, kv)
        bias = jnp.concatenate(
            [jnp.concatenate([bias_ref[kv * GQ + g, dr0 + 2 * j] for j in range(NA_KH // 2)], axis=1)
             for g in range(GQ)], axis=0)
        s_loc = _dot_nt(qg, _kv_slice(kwin, kv)) * ATTN_SCALE + bias
        s_ctx = _dot_nt(qg, _kv_slice(ck_ref, kv)) * ATTN_SCALE
        o = _softmax_pv([s_loc, s_ctx], [_kv_slice(vwin, kv), _kv_slice(cv_ref, kv)], None)
        o_ref[:, kv * GQ * HEAD_DIM:(kv + 1) * GQ * HEAD_DIM] = _unstack_heads(o, GRID_W).astype(bf16)


def _dec_specs(q_rows):
    q_off = T_CTX // q_rows
    per_batch = DEC_SEQ // q_rows
    kv_off = T_CTX // DEC_SEQ
    return [
        pl.BlockSpec((q_rows, D_MODEL), lambda b, n: (q_off + b * per_batch + n, 0)),
        pl.BlockSpec((DEC_SEQ, KV_DIM), lambda b, n: (kv_off + b, 0)),
        pl.BlockSpec((DEC_SEQ, KV_DIM), lambda b, n: (kv_off + b, 0)),
        pl.BlockSpec((None, PAST_LEN, KV_DIM), lambda b, n: (b, 0, 0)),
        pl.BlockSpec((None, PAST_LEN, KV_DIM), lambda b, n: (b, 0, 0)),
    ], pl.BlockSpec((q_rows, D_MODEL), lambda b, n: (b * per_batch + n, 0))


def na_attention(qb, kb, vb, ck, cv, bias2):
    in_specs, out_spec = _dec_specs(GRID_W)
    return pl.pallas_call(
        _na_attn_kernel,
        grid=(DEC_BATCH, DEC_SEQ // GRID_W),
        in_specs=in_specs + [pl.BlockSpec(memory_space=pltpu.VMEM)],
        out_specs=out_spec,
        out_shape=jax.ShapeDtypeStruct((T_DEC, D_MODEL), bf16),
        compiler_params=pltpu.CompilerParams(
            dimension_semantics=("arbitrary", "arbitrary"), vmem_limit_bytes=VMEM_LIMIT_ATTN),
        name="na_attention",
    )(qb, kb, vb, ck, cv, bias2)


def _swa_attn_kernel(sink_ref, q_ref, k_ref, v_ref, ck_ref, cv_ref, o_ref):
    n = pl.program_id(1)
    band = 3 * BLOCK
    start = pl.multiple_of(jnp.clip((n - 1) * BLOCK, 0, DEC_SEQ - band), BLOCK)
    kband = k_ref[pl.ds(start, band), :]
    vband = v_ref[pl.ds(start, band), :]
    rid = lax.broadcasted_iota(jnp.int32, (GQ * BLOCK, band), 0)
    cid = lax.broadcasted_iota(jnp.int32, (GQ * BLOCK, band), 1)
    q_pos = n * BLOCK + (rid & (BLOCK - 1))
    k_pos = start + cid
    allowed = jnp.abs(q_pos - k_pos) <= SWA_WINDOW
    for kv in range(N_KV_HEADS):
        qg = _stack_heads(q_ref, kv)
        s_loc = jnp.where(allowed, _dot_nt(qg, _kv_slice(kband, kv)) * ATTN_SCALE, NEG_INF)
        s_ctx = _dot_nt(qg, _kv_slice(ck_ref, kv)) * ATTN_SCALE
        o = _softmax_pv([s_loc, s_ctx], [_kv_slice(vband, kv), _kv_slice(cv_ref, kv)],
                        _sink_column(sink_ref, kv, BLOCK))
        o_ref[:, kv * GQ * HEAD_DIM:(kv + 1) * GQ * HEAD_DIM] = _unstack_heads(o, BLOCK).astype(bf16)


def swa_attention(qb, kb, vb, ck, cv, sink):
    in_specs, out_spec = _dec_specs(BLOCK)
    return pl.pallas_call(
        _swa_attn_kernel,
        grid=(DEC_BATCH, DEC_SEQ // BLOCK),
        in_specs=[pl.BlockSpec(memory_space=pltpu.SMEM)] + in_specs,
        out_specs=out_spec,
        out_shape=jax.ShapeDtypeStruct((T_DEC, D_MODEL), bf16),
        compiler_params=pltpu.CompilerParams(
            dimension_semantics=("arbitrary", "arbitrary"), vmem_limit_bytes=VMEM_LIMIT_ATTN),
        name="swa_attention",
    )(sink, qb, kb, vb, ck, cv)


def _dense_attn_kernel(q_ref, k_ref, v_ref, ck_ref, cv_ref, o_ref):
    for kv in range(N_KV_HEADS):
        qg = _stack_heads(q_ref, kv)
        s_lat = _dot_nt(qg, _kv_slice(k_ref, kv)) * ATTN_SCALE
        s_ctx = _dot_nt(qg, _kv_slice(ck_ref, kv)) * ATTN_SCALE
        o = _softmax_pv([s_lat, s_ctx], [_kv_slice(v_ref, kv), _kv_slice(cv_ref, kv)], None)
        o_ref[:, kv * GQ * HEAD_DIM:(kv + 1) * GQ * HEAD_DIM] = _unstack_heads(o, BLOCK).astype(bf16)


def dense_attention(qb, kb, vb, ck, cv):
    in_specs, out_spec = _dec_specs(BLOCK)
    return pl.pallas_call(
        _dense_attn_kernel,
        grid=(DEC_BATCH, DEC_SEQ // BLOCK),
        in_specs=in_specs,
        out_specs=out_spec,
        out_shape=jax.ShapeDtypeStruct((T_DEC, D_MODEL), bf16),
        compiler_params=pltpu.CompilerParams(
            dimension_semantics=("arbitrary", "arbitrary"), vmem_limit_bytes=VMEM_LIMIT_ATTN),
        name="dense_attention",
    )(qb, kb, vb, ck, cv)


def _topk_rows(s, ids, k):
    vals, labels = [], []
    for _ in range(k):
        m = jnp.max(s, axis=0, keepdims=True)
        lab = jnp.min(jnp.where(s == m, ids, 1e9), axis=0, keepdims=True)
        s = jnp.where(ids == lab, -jnp.inf, s)
        vals.append(m)
        labels.append(lab)
    return jnp.concatenate(vals, axis=0), jnp.concatenate(labels, axis=0)


def _pick(labels, table):
    out = jnp.zeros_like(labels)
    for a in range(PEER_TOPK):
        out = out + jnp.where(labels == float(a), table[a:a + 1], 0.0)
    return out


def _post_router_kernel(x_ref, o_ref, mod_ref, wo_ref, g_ref, wq_ref, keys_ref,
                        x1_ref, h2_ref, r8_ref, sh_ref, gate_ref, qp_scr):
    mod = mod_ref[...]
    g1 = mod[:, 2 * D_MODEL:3 * D_MODEL]
    sh2 = mod[:, 3 * D_MODEL:4 * D_MODEL]
    sc2 = mod[:, 4 * D_MODEL:5 * D_MODEL]
    x1 = x_ref[...] + g1 * _dot(o_ref[...], wo_ref[...])
    x1_ref[...] = x1
    h2 = _rmsnorm(x1, g_ref[...]) * (1.0 + sc2) + sh2
    h2_ref[...] = h2
    qp_scr[...] = _dot(h2.astype(bf16), wq_ref[...])

    key_ids = lax.broadcasted_iota(jnp.int32, (PEER_N_KEYS, LANES), 0).astype(f32)
    b_ids = lax.broadcasted_iota(jnp.int32, (PEER_TOPK, LANES), 0)
    cand_ids = jnp.concatenate(
        [(b_ids + a * PEER_TOPK).astype(f32) for a in range(8)]
        + [((b_ids[:8] + 8) * PEER_TOPK).astype(f32)], axis=0)
    n_chunks = TM_POST // LANES

    def head_chunk(hc, carry):
        h = hc // n_chunks
        c = hc % n_chunks
        rows = pl.ds(pl.multiple_of(c * LANES, LANES), LANES)
        tops = []
        for p in range(2):
            cols = pl.ds(pl.multiple_of((h * 2 + p) * PEER_DK, PEER_DK), PEER_DK)
            s = _dot_nt(keys_ref[h * 2 + p], qp_scr[rows, cols].astype(bf16))
            tops.append(_topk_rows(s, key_ids, PEER_TOPK))
        (s1, i1), (s2, i2) = tops
        pieces = [jnp.where(b_ids < PEER_TOPK // (a + 1), s1[a:a + 1] + s2, -jnp.inf) for a in range(8)]
        pieces.append(s1[8:] + s2[0:1])
        top_s, pos = _topk_rows(jnp.concatenate(pieces, axis=0), cand_ids, PEER_TOPK)
        a_lab = jnp.floor(pos * (1.0 / PEER_TOPK))
        b_lab = pos - a_lab * PEER_TOPK
        idx = (_pick(a_lab, i1) * PEER_N_KEYS + _pick(b_lab, i2)).astype(jnp.int32)
        e = jnp.exp(top_s - top_s[0:1])
        gate = e / jnp.sum(e, axis=0, keepdims=True)
        out_rows = pl.ds(pl.multiple_of(h * PEER_TOPK, PEER_TOPK), PEER_TOPK)
        r8_ref[out_rows, rows] = (idx & (HALF_EXPERTS - 1)) * SUBLANES
        sh_ref[out_rows, rows] = 16 - ((idx >> 13) << 4)
        gate_ref[out_rows, rows] = gate
        return carry

    lax.fori_loop(0, PEER_HEADS * n_chunks, head_chunk, 0)


def post_router(x, o, mod3, wo, g, wq, keys):
    tm = TM_POST
    row = lambda i: (i, 0)
    col = lambda i: (0, i)
    full = lambda i: (0, 0)
    return pl.pallas_call(
        _post_router_kernel,
        grid=(T_ALL // tm,),
        in_specs=[
            pl.BlockSpec((tm, D_MODEL), row),
            pl.BlockSpec((tm, D_MODEL), row),
            pl.BlockSpec((None, 1, 6 * D_MODEL), _mod_row_map(tm)),
            pl.BlockSpec((D_MODEL, D_MODEL), full),
            pl.BlockSpec((1, D_MODEL), full),
            pl.BlockSpec((D_MODEL, 2 * PEER_HEADS * PEER_DK), full),
            pl.BlockSpec((2 * PEER_HEADS, PEER_N_KEYS, PEER_DK), lambda i: (0, 0, 0)),
        ],
        out_specs=[
            pl.BlockSpec((tm, D_MODEL), row),
            pl.BlockSpec((tm, D_MODEL), row),
            pl.BlockSpec((N_PAIRS, tm), col),
            pl.BlockSpec((N_PAIRS, tm), col),
            pl.BlockSpec((N_PAIRS, tm), col),
        ],
        out_shape=[
            jax.ShapeDtypeStruct((T_ALL, D_MODEL), f32),
            jax.ShapeDtypeStruct((T_ALL, D_MODEL), f32),
            jax.ShapeDtypeStruct((N_PAIRS, T_ALL), jnp.int32),
            jax.ShapeDtypeStruct((N_PAIRS, T_ALL), jnp.int32),
            jax.ShapeDtypeStruct((N_PAIRS, T_ALL), f32),
        ],
        scratch_shapes=[pltpu.VMEM((tm, 2 * PEER_HEADS * PEER_DK), f32)],
        compiler_params=pltpu.CompilerParams(dimension_semantics=("arbitrary",)),
        name="post_router",
    )(x, o, mod3, wo, g, wq, keys)


def _unpack_expert(tab_ref, r8, sh):
    word = tab_ref[pl.ds(pl.multiple_of(r8, SUBLANES), SUBLANES), :]
    return pltpu.bitcast((word << sh.astype(jnp.uint32)) & jnp.uint32(0xFFFF0000), f32)


def _sublane_sums(ps):
    sub = lax.broadcasted_iota(jnp.int32, (SUBLANES, LANES), 0)
    m4 = sub < 4
    m2 = (sub & 2) == 0
    m1 = (sub & 1) == 0
    lvl1 = [jnp.where(m4, ps[a], ps[b]) + pltpu.roll(jnp.where(m4, ps[b], ps[a]), 4, 0)
            for a, b in ((0, 4), (2, 6), (1, 5), (3, 7))]
    lvl2 = [jnp.where(m2, x, pltpu.roll(y, 2, 0)) + jnp.where(m2, pltpu.roll(x, 6, 0), y)
            for x, y in ((lvl1[0], lvl1[1]), (lvl1[2], lvl1[3]))]
    x, y = lvl2
    return jnp.where(m1, x, pltpu.roll(y, 1, 0)) + jnp.where(m1, pltpu.roll(x, 7, 0), y)


def _expert_u_kernel(r8_ref, sh_ref, x_ref, gate_ref, tab_ref, fsel_ref, w_ref, part_scr):
    def token(t, carry):
        xt = x_ref[t]
        lanes = pl.ds(pl.multiple_of(t * LANES, LANES), LANES)
        for j in range(N_PAIRS // SUBLANES):
            ps = [_unpack_expert(tab_ref, r8_ref[t, j * SUBLANES + s], sh_ref[t, j * SUBLANES + s]) * xt
                  for s in range(SUBLANES)]
            part_scr[j * SUBLANES:(j + 1) * SUBLANES, lanes] = _sublane_sums(ps)
        return carry

    lax.fori_loop(0, TB, token, 0)
    hi, lo = _split(part_scr[...])
    fsel = fsel_ref[...]
    a = _dot(hi, fsel) + _dot(lo, fsel)
    gelu = 0.5 * a * (1.0 + lax.erf(a * np.float32(2.0 ** -0.5)))
    w_ref[...] = gelu * gate_ref[...]


def expert_u(r8_t, sh_t, x3, gate, tab, fsel):
    tok = lambda i: (i, 0)
    col = lambda i: (0, i)
    return pl.pallas_call(
        _expert_u_kernel,
        grid=(T_ALL // TB,),
        in_specs=[
            pl.BlockSpec((TB, N_PAIRS), tok, memory_space=pltpu.SMEM),
            pl.BlockSpec((TB, N_PAIRS), tok, memory_space=pltpu.SMEM),
            pl.BlockSpec((TB, SUBLANES, LANES), lambda i: (i, 0, 0)),
            pl.BlockSpec((N_PAIRS, TB), col),
            pl.BlockSpec(memory_space=pltpu.VMEM),
            pl.BlockSpec(memory_space=pltpu.VMEM),
        ],
        out_specs=pl.BlockSpec((N_PAIRS, TB), col),
        out_shape=jax.ShapeDtypeStruct((N_PAIRS, T_ALL), f32),
        scratch_shapes=[pltpu.VMEM((N_PAIRS, TB * LANES), f32)],
        compiler_params=pltpu.CompilerParams(
            dimension_semantics=("arbitrary",), vmem_limit_bytes=VMEM_LIMIT_EXPERT),
        name="expert_u",
    )(r8_t, sh_t, x3, gate, tab, fsel)


def _expert_v_kernel(r8_ref, sh_ref, w_ref, x1_ref, g2_ref, tab_ref, o_ref):
    g2 = g2_ref[0]

    def token(t, carry):
        accs = [jnp.zeros((SUBLANES, LANES), f32) for _ in range(4)]
        for k in range(N_PAIRS):
            val = _unpack_expert(tab_ref, r8_ref[t, k], sh_ref[t, k])
            accs[k % 4] = accs[k % 4] + w_ref[t, k] * val
        o_ref[t] = x1_ref[t] + g2 * ((accs[0] + accs[1]) + (accs[2] + accs[3]))
        return carry

    lax.fori_loop(0, TB, token, 0)


def expert_v(r8_t, sh_t, w_t, x1_3, g2_3, tab):
    tok = lambda i: (i, 0)
    tok3 = lambda i: (i, 0, 0)
    return pl.pallas_call(
        _expert_v_kernel,
        grid=(T_ALL // TB,),
        in_specs=[
            pl.BlockSpec((TB, N_PAIRS), tok, memory_space=pltpu.SMEM),
            pl.BlockSpec((TB, N_PAIRS), tok, memory_space=pltpu.SMEM),
            pl.BlockSpec((TB, N_PAIRS), tok, memory_space=pltpu.SMEM),
            pl.BlockSpec((TB, SUBLANES, LANES), tok3),
            pl.BlockSpec((1, SUBLANES, LANES), _mod_row_map(TB)),
            pl.BlockSpec(memory_space=pltpu.VMEM),
        ],
        out_specs=pl.BlockSpec((TB, SUBLANES, LANES), tok3),
        out_shape=jax.ShapeDtypeStruct((T_ALL, SUBLANES, LANES), f32),
        compiler_params=pltpu.CompilerParams(
            dimension_semantics=("arbitrary",), vmem_limit_bytes=VMEM_LIMIT_EXPERT),
        name="expert_v",
    )(r8_t, sh_t, w_t, x1_3, g2_3, tab)


def _final_norm_kernel(x_ref, g_ref, o_ref):
    o_ref[...] = _rmsnorm(x_ref[...], g_ref[...])


def final_norm(x, g):
    tm = TM_QKV
    return pl.pallas_call(
        _final_norm_kernel,
        grid=(T_ALL // tm,),
        in_specs=[pl.BlockSpec((tm, D_MODEL), lambda i: (i, 0)), pl.BlockSpec((1, D_MODEL), lambda i: (0, 0))],
        out_specs=pl.BlockSpec((tm, D_MODEL), lambda i: (i, 0)),
        out_shape=jax.ShapeDtypeStruct((T_ALL, D_MODEL), f32),
        compiler_params=pltpu.CompilerParams(dimension_semantics=("arbitrary",)),
        name="final_norm",
    )(x, g)


def _rope_tables():
    pos = np.arange(DEC_SEQ)
    d = np.arange(LANES) % HEAD_DIM
    quarter = HEAD_DIM // 4
    freq = ROPE_THETA ** (-(d % quarter).astype(np.float64) / quarter)
    coord = np.where((d < HEAD_DIM // 2)[None, :], (pos // GRID_W)[:, None], (pos % GRID_W)[:, None])
    ang = coord.astype(np.float64) * freq[None, :]
    first = ((d % (HEAD_DIM // 2)) < quarter)[None, :]
    cos = np.cos(ang)
    sin = np.sin(ang)
    return (jnp.asarray(cos, f32), jnp.asarray(np.where(first, -sin, 0.0), f32),
            jnp.asarray(np.where(first, 0.0, sin), f32))


def _na_bias_pairs(rpb):
    col = np.arange(GRID_W)
    col_start = np.clip(col - NA_KW // 2, 0, GRID_W - NA_KW)
    c2 = np.arange(GRID_W)
    inside = (c2[None, :] >= col_start[:, None]) & (c2[None, :] < col_start[:, None] + NA_KW)
    dc = np.clip(c2[None, :] - col[:, None] + (NA_KW - 1), 0, 2 * NA_KW - 2)
    full = jnp.where(jnp.asarray(inside)[None, None], rpb[:, :, dc], NEG_INF)
    return jnp.concatenate([full[:, :-1], full[:, 1:]], axis=-1)


def _pack_table(t):
    bits = lax.bitcast_convert_type(t.astype(bf16), jnp.uint16).astype(jnp.uint32)
    packed = bits[:HALF_EXPERTS] | (bits[HALF_EXPERTS:] << 16)
    return packed.reshape(HALF_EXPERTS * SUBLANES, LANES)


def kernel(x_prompt, x_sample, cache_k, cache_v, c, c_ctx, w_ada, b_ada, norm_g, final_g, w_qkv, w_o, na_rpb, swa_sink, qk_gain, peer_wq, peer_keys, peer_u, peer_v):
    cvec = jnp.concatenate([c_ctx[None], c, jnp.zeros((N_MOD_ROWS - 1 - DEC_BATCH, D_MODEL), f32)], axis=0)
    mods = ada_mods(cvec, w_ada, b_ada)
    x = jnp.concatenate([x_prompt.reshape(T_CTX, D_MODEL), x_sample.reshape(T_DEC, D_MODEL)], axis=0)

    rope_tabs = _rope_tables()
    head_sum = jnp.asarray(np.kron(np.eye(N_HEADS), np.ones((HEAD_DIM, HEAD_DIM))), bf16)
    fsel = jnp.asarray(np.kron(np.eye(TB), np.ones((LANES, 1))), bf16)

    ks, vs = [], []
    for l in range(DEPTH):
        kind, j = l % 3, l // 3
        mod3 = mods[l].reshape(N_MOD_ROWS, 1, 6 * D_MODEL)
        norm_args = None
        if kind == 2:
            norm_args = (head_sum, jnp.tile(qk_gain[j, 0], N_HEADS)[None], jnp.tile(qk_gain[j, 1], N_KV_HEADS)[None])
        rope_args = rope_tabs if kind != 0 else None
        qb, k, v, kb, vb = qkv_project(x, mod3, norm_g[l, 0][None], w_qkv[l].astype(bf16), norm_args, rope_args)
        ks.append(k[:T_CTX].reshape(BATCH, SEQ, N_KV_HEADS, HEAD_DIM))
        vs.append(v[:T_CTX].reshape(BATCH, SEQ, N_KV_HEADS, HEAD_DIM))

        ck = cache_k[:, l].reshape(DEC_BATCH, PAST_LEN, KV_DIM).astype(bf16)
        cv = cache_v[:, l].reshape(DEC_BATCH, PAST_LEN, KV_DIM).astype(bf16)
        o_ctx = ctx_attention(qb, kb, vb, swa_sink[j] if kind == 1 else None)
        if kind == 0:
            o_dec = na_attention(qb, kb, vb, ck, cv, _na_bias_pairs(na_rpb[j]))
        elif kind == 1:
            o_dec = swa_attention(qb, kb, vb, ck, cv, swa_sink[j])
        else:
            o_dec = dense_attention(qb, kb, vb, ck, cv)
        o = jnp.concatenate([o_ctx, o_dec], axis=0)

        x1, h2, r8, sh, gate = post_router(
            x, o, mod3, w_o[l].astype(bf16), norm_g[l, 1][None], peer_wq[l].astype(bf16),
            peer_keys[l].reshape(2 * PEER_HEADS, PEER_N_KEYS, PEER_DK).astype(bf16))
        r8_t, sh_t = r8.T, sh.T
        w = expert_u(r8_t, sh_t, h2.reshape(T_ALL, SUBLANES, LANES), gate, _pack_table(peer_u[l]), fsel)
        g2 = mods[l][:, 5 * D_MODEL:].reshape(N_MOD_ROWS, SUBLANES, LANES)
        x = expert_v(r8_t, sh_t, w.T, x1.reshape(T_ALL, SUBLANES, LANES), g2,
                     _pack_table(peer_v[l])).reshape(T_ALL, D_MODEL)

    y = final_norm(x, final_g[None])
    y_prompt = y[:T_CTX].reshape(BATCH, SEQ, D_MODEL)
    y_sample = y[T_CTX:].reshape(DEC_BATCH, DEC_SEQ, D_MODEL)
    return (y_prompt, y_sample, jnp.stack(ks, axis=1), jnp.stack(vs, axis=1))
```

```python
import functools

import numpy as np
import jax
import jax.numpy as jnp
from jax import lax
from jax.experimental import pallas as pl
from jax.experimental.pallas import tpu as pltpu

f32 = jnp.float32
bf16 = jnp.bfloat16

D_MODEL = 1024
BATCH = 32
SEQ = 256
DEPTH = 4
DEC_BATCH = 2
DEC_SEQ = 4096
PAST_LEN = 256
GRID_W = 64
N_HEADS = 16
N_KV_HEADS = 4
HEAD_DIM = 64
GQ = N_HEADS // N_KV_HEADS
KV_DIM = N_KV_HEADS * HEAD_DIM
ATTN_SCALE = HEAD_DIM ** -0.5
BLOCK = 128
SWA_WINDOW = 128
NA_KH = 8
NA_KW = 16
ROPE_THETA = 10000.0
NORM_EPS = 1e-6
NEG_INF = -1e30
PEER_HEADS = 8
PEER_N_KEYS = 128
PEER_TOPK = 16
PEER_DK = 128

T_CTX = BATCH * SEQ
T_DEC = DEC_BATCH * DEC_SEQ
T_ALL = T_CTX + T_DEC
N_MOD_ROWS = 8
N_PAIRS = PEER_HEADS * PEER_TOPK
HALF_EXPERTS = PEER_N_KEYS * PEER_N_KEYS // 2
SUBLANES = 8
LANES = 128
TB = 128
TM_QKV = 512
TM_POST = 256
VMEM_LIMIT_EXPERT = 56 * 1024 * 1024
VMEM_LIMIT_ATTN = 48 * 1024 * 1024


def _dot(a, b):
    return jnp.dot(a, b, preferred_element_type=f32)


def _dot_nt(a, b):
    return lax.dot_general(a, b, (((1,), (1,)), ((), ())), preferred_element_type=f32)


def _split(a):
    hi = a.astype(bf16)
    return hi, (a - hi.astype(f32)).astype(bf16)


def _rmsnorm(x, g):
    return x * lax.rsqrt(jnp.mean(x * x, axis=-1, keepdims=True) + NORM_EPS) * g


def _mod_row_map(rows_per_block):
    ctx_blocks = T_CTX // rows_per_block
    per_batch = DEC_SEQ // rows_per_block
    return lambda i: (jnp.where(i < ctx_blocks, 0, 1 + (i - ctx_blocks) // per_batch), 0, 0)


def _ada_kernel(c_ref, w_ref, b_ref, o_ref):
    c = c_ref[...]
    s = c * (1.0 / (1.0 + jnp.exp(-c)))
    sh, sl = _split(s)
    wh, wl = _split(w_ref[0])
    o_ref[0] = _dot(sh, wh) + _dot(sh, wl) + _dot(sl, wh) + b_ref[0]


def ada_mods(cvec, w_ada, b_ada):
    n_out = 6 * D_MODEL
    tn = 1536
    return pl.pallas_call(
        _ada_kernel,
        grid=(DEPTH, n_out // tn),
        in_specs=[
            pl.BlockSpec((N_MOD_ROWS, D_MODEL), lambda l, j: (0, 0)),
            pl.BlockSpec((1, D_MODEL, tn), lambda l, j: (l, 0, j)),
            pl.BlockSpec((1, 1, tn), lambda l, j: (l, 0, j)),
        ],
        out_specs=pl.BlockSpec((1, N_MOD_ROWS, tn), lambda l, j: (l, 0, j)),
        out_shape=jax.ShapeDtypeStruct((DEPTH, N_MOD_ROWS, n_out), f32),
        compiler_params=pltpu.CompilerParams(dimension_semantics=("arbitrary", "arbitrary")),
        name="ada_mods",
    )(cvec, w_ada, b_ada.reshape(DEPTH, 1, n_out))


def _head_norm(x, bd, gain):
    hi, lo = _split(x * x)
    ms = (_dot(hi, bd) + _dot(lo, bd)) * (1.0 / HEAD_DIM)
    return x * lax.rsqrt(ms + NORM_EPS) * gain


def _rope(x, cos, sin_up, sin_dn):
    return (x * cos + pltpu.roll(x, LANES - 16, 1) * sin_up + pltpu.roll(x, 16, 1) * sin_dn)


def _qkv_kernel(*refs, norm_qk, rope):
    refs = list(refs)
    x_ref, mod_ref, g_ref, w_ref = refs[:4]
    refs = refs[4:]
    if norm_qk:
        bd_ref, gq_ref, gk_ref = refs[:3]
        refs = refs[3:]
    if rope:
        cos_ref, sup_ref, sdn_ref = refs[:3]
        refs = refs[3:]
    q_ref, k_ref, v_ref, kb_ref, vb_ref = refs

    mod = mod_ref[...]
    sh1 = mod[:, 0:D_MODEL]
    sc1 = mod[:, D_MODEL:2 * D_MODEL]
    h = _rmsnorm(x_ref[...], g_ref[...]) * (1.0 + sc1) + sh1
    qkv = _dot(h.astype(bf16), w_ref[...])
    q = qkv[:, :D_MODEL]
    k = qkv[:, D_MODEL:D_MODEL + KV_DIM]
    v = qkv[:, D_MODEL + KV_DIM:]
    if norm_qk:
        q = _head_norm(q, bd_ref[...], gq_ref[...])
        k = _head_norm(k, bd_ref[:KV_DIM, :KV_DIM], gk_ref[...])
    k_ref[...] = k
    v_ref[...] = v
    vb_ref[...] = v.astype(bf16)
    if rope:
        is_dec = pl.program_id(0) >= T_CTX // TM_QKV
        cos = jnp.where(is_dec, cos_ref[...], 1.0)
        sup = jnp.where(is_dec, sup_ref[...], 0.0)
        sdn = jnp.where(is_dec, sdn_ref[...], 0.0)
        for j in range(D_MODEL // LANES):
            sl = slice(j * LANES, (j + 1) * LANES)
            q_ref[:, sl] = _rope(q[:, sl], cos, sup, sdn).astype(bf16)
        for j in range(KV_DIM // LANES):
            sl = slice(j * LANES, (j + 1) * LANES)
            kb_ref[:, sl] = _rope(k[:, sl], cos, sup, sdn).astype(bf16)
    else:
        q_ref[...] = q.astype(bf16)
        kb_ref[...] = k.astype(bf16)


def qkv_project(x, mod3, g, w, norm_args, rope_args):
    tm = TM_QKV
    n_blocks = T_ALL // tm
    ctx_blocks = T_CTX // tm
    per_batch = DEC_SEQ // tm
    row = lambda i: (i, 0)
    full = lambda i: (0, 0)
    in_specs = [
        pl.BlockSpec((tm, D_MODEL), row),
        pl.BlockSpec((None, 1, 6 * D_MODEL), _mod_row_map(tm)),
        pl.BlockSpec((1, D_MODEL), full),
        pl.BlockSpec((D_MODEL, D_MODEL + 2 * KV_DIM), full),
    ]
    args = [x, mod3, g, w]
    if norm_args is not None:
        in_specs += [pl.BlockSpec((D_MODEL, D_MODEL), full),
                     pl.BlockSpec((1, D_MODEL), full), pl.BlockSpec((1, KV_DIM), full)]
        args += list(norm_args)
    if rope_args is not None:
        pos_map = lambda i: (jnp.maximum(i - ctx_blocks, 0) % per_batch, 0)
        in_specs += [pl.BlockSpec((tm, LANES), pos_map)] * 3
        args += list(rope_args)
    return pl.pallas_call(
        functools.partial(_qkv_kernel, norm_qk=norm_args is not None, rope=rope_args is not None),
        grid=(n_blocks,),
        in_specs=in_specs,
        out_specs=[
            pl.BlockSpec((tm, D_MODEL), row),
            pl.BlockSpec((tm, KV_DIM), row),
            pl.BlockSpec((tm, KV_DIM), row),
            pl.BlockSpec((tm, KV_DIM), row),
            pl.BlockSpec((tm, KV_DIM), row),
        ],
        out_shape=[
            jax.ShapeDtypeStruct((T_ALL, D_MODEL), bf16),
            jax.ShapeDtypeStruct((T_ALL, KV_DIM), f32),
            jax.ShapeDtypeStruct((T_ALL, KV_DIM), f32),
            jax.ShapeDtypeStruct((T_ALL, KV_DIM), bf16),
            jax.ShapeDtypeStruct((T_ALL, KV_DIM), bf16),
        ],
        compiler_params=pltpu.CompilerParams(dimension_semantics=("arbitrary",)),
        name="qkv_project",
    )(*args)


def _stack_heads(q_ref, kv):
    return jnp.concatenate(
        [q_ref[:, (kv * GQ + g) * HEAD_DIM:(kv * GQ + g + 1) * HEAD_DIM] for g in range(GQ)], axis=0)


def _unstack_heads(o, rows):
    return jnp.concatenate([o[g * rows:(g + 1) * rows] for g in range(GQ)], axis=1)


def _sink_column(sink_ref, kv, rows):
    rid = lax.broadcasted_iota(jnp.int32, (GQ * rows, 1), 0)
    col = jnp.full((GQ * rows, 1), sink_ref[kv * GQ + GQ - 1], f32)
    for g in range(GQ - 2, -1, -1):
        col = jnp.where(rid < (g + 1) * rows, sink_ref[kv * GQ + g], col)
    return col


def _softmax_pv(scores, values, sink_col):
    m = functools.reduce(jnp.maximum, [jnp.max(s, axis=-1, keepdims=True) for s in scores])
    if sink_col is not None:
        m = jnp.maximum(m, sink_col)
    ps = [jnp.exp(s - m) for s in scores]
    d = functools.reduce(lambda a, b: a + b, [jnp.sum(p, axis=-1, keepdims=True) for p in ps])
    if sink_col is not None:
        d = d + jnp.exp(sink_col - m)
    o = functools.reduce(lambda a, b: a + b, [_dot(p.astype(bf16), v) for p, v in zip(ps, values)])
    return o / d


def _kv_slice(ref_or_val, kv):
    return ref_or_val[:, kv * HEAD_DIM:(kv + 1) * HEAD_DIM]


def _ctx_attn_kernel(*refs, has_sink):
    if has_sink:
        sink_ref, q_ref, k_ref, v_ref, o_ref = refs
    else:
        q_ref, k_ref, v_ref, o_ref = refs
    for kv in range(N_KV_HEADS):
        qg = _stack_heads(q_ref, kv)
        s = _dot_nt(qg, _kv_slice(k_ref, kv)) * ATTN_SCALE
        sink_col = _sink_column(sink_ref, kv, SEQ) if has_sink else None
        o = _softmax_pv([s], [_kv_slice(v_ref, kv)], sink_col)
        o_ref[:, kv * GQ * HEAD_DIM:(kv + 1) * GQ * HEAD_DIM] = _unstack_heads(o, SEQ).astype(bf16)


def ctx_attention(qb, kb, vb, sink):
    row = lambda b: (b, 0)
    in_specs = [pl.BlockSpec((SEQ, D_MODEL), row), pl.BlockSpec((SEQ, KV_DIM), row),
                pl.BlockSpec((SEQ, KV_DIM), row)]
    args = [qb, kb, vb]
    if sink is not None:
        in_specs = [pl.BlockSpec(memory_space=pltpu.SMEM)] + in_specs
        args = [sink] + args
    return pl.pallas_call(
        functools.partial(_ctx_attn_kernel, has_sink=sink is not None),
        grid=(BATCH,),
        in_specs=in_specs,
        out_specs=pl.BlockSpec((SEQ, D_MODEL), row),
        out_shape=jax.ShapeDtypeStruct((T_CTX, D_MODEL), bf16),
        compiler_params=pltpu.CompilerParams(dimension_semantics=("arbitrary",)),
        name="ctx_attention",
    )(*args)


def _na_attn_kernel(q_ref, k_ref, v_ref, ck_ref, cv_ref, bias_ref, o_ref):
    r = pl.program_id(1)
    rows = DEC_SEQ // GRID_W
    rs = jnp.clip(r - NA_KH // 2, 0, rows - NA_KH)
    dr0 = rs - r + (NA_KH - 1)
    start = pl.multiple_of(rs * GRID_W, GRID_W)
    kwin = k_ref[pl.ds(start, NA_KH * GRID_W), :]
    vwin = v_ref[pl.ds(start, NA_KH * GRID_W), :]
    for kv in range(N_KV_HEADS):
        qg = _stack_heads(q_ref, kv)
        bias = jnp.concatenate(
            [jnp.concatenate([bias_ref[kv * GQ + g, dr0 + 2 * j] for j in range(NA_KH // 2)], axis=1)
             for g in range(GQ)], axis=0)
        s_loc = _dot_nt(qg, _kv_slice(kwin, kv)) * ATTN_SCALE + bias
        s_ctx = _dot_nt(qg, _kv_slice(ck_ref, kv)) * ATTN_SCALE
        o = _softmax_pv([s_loc, s_ctx], [_kv_slice(vwin, kv), _kv_slice(cv_ref, kv)], None)
        o_ref[:, kv * GQ * HEAD_DIM:(kv + 1) * GQ * HEAD_DIM] = _unstack_heads(o, GRID_W).astype(bf16)


def _dec_specs(q_rows):
    q_off = T_CTX // q_rows
    per_batch = DEC_SEQ // q_rows
    kv_off = T_CTX // DEC_SEQ
    return [
        pl.BlockSpec((q_rows, D_MODEL), lambda b, n: (q_off + b * per_batch + n, 0)),
        pl.BlockSpec((DEC_SEQ, KV_DIM), lambda b, n: (kv_off + b, 0)),
        pl.BlockSpec((DEC_SEQ, KV_DIM), lambda b, n: (kv_off + b, 0)),
        pl.BlockSpec((None, PAST_LEN, KV_DIM), lambda b, n: (b, 0, 0)),
        pl.BlockSpec((None, PAST_LEN, KV_DIM), lambda b, n: (b, 0, 0)),
    ], pl.BlockSpec((q_rows, D_MODEL), lambda b, n: (b * per_batch + n, 0))


def na_attention(qb, kb, vb, ck, cv, bias2):
    in_specs, out_spec = _dec_specs(GRID_W)
    return pl.pallas_call(
        _na_attn_kernel,
        grid=(DEC_BATCH, DEC_SEQ // GRID_W),
        in_specs=in_specs + [pl.BlockSpec(memory_space=pltpu.VMEM)],
        out_specs=out_spec,
        out_shape=jax.ShapeDtypeStruct((T_DEC, D_MODEL), bf16),
        compiler_params=pltpu.CompilerParams(
            dimension_semantics=("arbitrary", "arbitrary"), vmem_limit_bytes=VMEM_LIMIT_ATTN),
        name="na_attention",
    )(qb, kb, vb, ck, cv, bias2)


def _swa_attn_kernel(sink_ref, q_ref, k_ref, v_ref, ck_ref, cv_ref, o_ref):
    n = pl.program_id(1)
    band = 3 * BLOCK
    start = pl.multiple_of(jnp.clip((n - 1) * BLOCK, 0, DEC_SEQ - band), BLOCK)
    kband = k_ref[pl.ds(start, band), :]
    vband = v_ref[pl.ds(start, band), :]
    rid = lax.broadcasted_iota(jnp.int32, (GQ * BLOCK, band), 0)
    cid = lax.broadcasted_iota(jnp.int32, (GQ * BLOCK, band), 1)
    q_pos = n * BLOCK + (rid & (BLOCK - 1))
    k_pos = start + cid
    allowed = jnp.abs(q_pos - k_pos) <= SWA_WINDOW
    for kv in range(N_KV_HEADS):
        qg = _stack_heads(q_ref, kv)
        s_loc = jnp.where(allowed, _dot_nt(qg, _kv_slice(kband, kv)) * ATTN_SCALE, NEG_INF)
        s_ctx = _dot_nt(qg, _kv_slice(ck_ref, kv)) * ATTN_SCALE
        o = _softmax_pv([s_loc, s_ctx], [_kv_slice(vband, kv), _kv_slice(cv_ref, kv)],
                        _sink_column(sink_ref, kv, BLOCK))
        o_ref[:, kv * GQ * HEAD_DIM:(kv + 1) * GQ * HEAD_DIM] = _unstack_heads(o, BLOCK).astype(bf16)


def swa_attention(qb, kb, vb, ck, cv, sink):
    in_specs, out_spec = _dec_specs(BLOCK)
    return pl.pallas_call(
        _swa_attn_kernel,
        grid=(DEC_BATCH, DEC_SEQ // BLOCK),
        in_specs=[pl.BlockSpec(memory_space=pltpu.SMEM)] + in_specs,
        out_specs=out_spec,
        out_shape=jax.ShapeDtypeStruct((T_DEC, D_MODEL), bf16),
        compiler_params=pltpu.CompilerParams(
            dimension_semantics=("arbitrary", "arbitrary"), vmem_limit_bytes=VMEM_LIMIT_ATTN),
        name="swa_attention",
    )(sink, qb, kb, vb, ck, cv)


def _dense_attn_kernel(q_ref, k_ref, v_ref, ck_ref, cv_ref, o_ref):
    for kv in range(N_KV_HEADS):
        qg = _stack_heads(q_ref, kv)
        s_lat = _dot_nt(qg, _kv_slice(k_ref, kv)) * ATTN_SCALE
        s_ctx = _dot_nt(qg, _kv_slice(ck_ref, kv)) * ATTN_SCALE
        o = _softmax_pv([s_lat, s_ctx], [_kv_slice(v_ref, kv), _kv_slice(cv_ref, kv)], None)
        o_ref[:, kv * GQ * HEAD_DIM:(kv + 1) * GQ * HEAD_DIM] = _unstack_heads(o, BLOCK).astype(bf16)


def dense_attention(qb, kb, vb, ck, cv):
    in_specs, out_spec = _dec_specs(BLOCK)
    return pl.pallas_call(
        _dense_attn_kernel,
        grid=(DEC_BATCH, DEC_SEQ // BLOCK),
        in_specs=in_specs,
        out_specs=out_spec,
        out_shape=jax.ShapeDtypeStruct((T_DEC, D_MODEL), bf16),
        compiler_params=pltpu.CompilerParams(
            dimension_semantics=("arbitrary", "arbitrary"), vmem_limit_bytes=VMEM_LIMIT_ATTN),
        name="dense_attention",
    )(qb, kb, vb, ck, cv)


def _topk_rows(s, ids, k):
    vals, labels = [], []
    for _ in range(k):
        m = jnp.max(s, axis=0, keepdims=True)
        lab = jnp.min(jnp.where(s == m, ids, 1e9), axis=0, keepdims=True)
        s = jnp.where(ids == lab, -jnp.inf, s)
        vals.append(m)
        labels.append(lab)
    return jnp.concatenate(vals, axis=0), jnp.concatenate(labels, axis=0)


def _pick(labels, table):
    out = jnp.zeros_like(labels)
    for a in range(PEER_TOPK):
        out = out + jnp.where(labels == float(a), table[a:a + 1], 0.0)
    return out


def _post_router_kernel(x_ref, o_ref, mod_ref, wo_ref, g_ref, wq_ref, keys_ref,
                        x1_ref, h2_ref, r8_ref, sh_ref, gate_ref, qp_scr):
    mod = mod_ref[...]
    g1 = mod[:, 2 * D_MODEL:3 * D_MODEL]
    sh2 = mod[:, 3 * D_MODEL:4 * D_MODEL]
    sc2 = mod[:, 4 * D_MODEL:5 * D_MODEL]
    x1 = x_ref[...] + g1 * _dot(o_ref[...], wo_ref[...])
    x1_ref[...] = x1
    h2 = _rmsnorm(x1, g_ref[...]) * (1.0 + sc2) + sh2
    h2_ref[...] = h2
    qp_scr[...] = _dot(h2.astype(bf16), wq_ref[...])

    key_ids = lax.broadcasted_iota(jnp.int32, (PEER_N_KEYS, LANES), 0).astype(f32)
    b_ids = lax.broadcasted_iota(jnp.int32, (PEER_TOPK, LANES), 0)
    cand_ids = jnp.concatenate(
        [(b_ids + a * PEER_TOPK).astype(f32) for a in range(8)]
        + [((b_ids[:8] + 8) * PEER_TOPK).astype(f32)], axis=0)
    n_chunks = TM_POST // LANES

    def head_chunk(hc, carry):
        h = hc // n_chunks
        c = hc % n_chunks
        rows = pl.ds(pl.multiple_of(c * LANES, LANES), LANES)
        tops = []
        for p in range(2):
            cols = pl.ds(pl.multiple_of((h * 2 + p) * PEER_DK, PEER_DK), PEER_DK)
            s = _dot_nt(keys_ref[h * 2 + p], qp_scr[rows, cols].astype(bf16))
            tops.append(_topk_rows(s, key_ids, PEER_TOPK))
        (s1, i1), (s2, i2) = tops
        pieces = [jnp.where(b_ids < PEER_TOPK // (a + 1), s1[a:a + 1] + s2, -jnp.inf) for a in range(8)]
        pieces.append(s1[8:] + s2[0:1])
        top_s, pos = _topk_rows(jnp.concatenate(pieces, axis=0), cand_ids, PEER_TOPK)
        a_lab = jnp.floor(pos * (1.0 / PEER_TOPK))
        b_lab = pos - a_lab * PEER_TOPK
        idx = (_pick(a_lab, i1) * PEER_N_KEYS + _pick(b_lab, i2)).astype(jnp.int32)
        e = jnp.exp(top_s - top_s[0:1])
        gate = e / jnp.sum(e, axis=0, keepdims=True)
        out_rows = pl.ds(pl.multiple_of(h * PEER_TOPK, PEER_TOPK), PEER_TOPK)
        r8_ref[out_rows, rows] = (idx & (HALF_EXPERTS - 1)) * SUBLANES
        sh_ref[out_rows, rows] = 16 - ((idx >> 13) << 4)
        gate_ref[out_rows, rows] = gate
        return carry

    lax.fori_loop(0, PEER_HEADS * n_chunks, head_chunk, 0)


def post_router(x, o, mod3, wo, g, wq, keys):
    tm = TM_POST
    row = lambda i: (i, 0)
    col = lambda i: (0, i)
    full = lambda i: (0, 0)
    return pl.pallas_call(
        _post_router_kernel,
        grid=(T_ALL // tm,),
        in_specs=[
            pl.BlockSpec((tm, D_MODEL), row),
            pl.BlockSpec((tm, D_MODEL), row),
            pl.BlockSpec((None, 1, 6 * D_MODEL), _mod_row_map(tm)),
            pl.BlockSpec((D_MODEL, D_MODEL), full),
            pl.BlockSpec((1, D_MODEL), full),
            pl.BlockSpec((D_MODEL, 2 * PEER_HEADS * PEER_DK), full),
            pl.BlockSpec((2 * PEER_HEADS, PEER_N_KEYS, PEER_DK), lambda i: (0, 0, 0)),
        ],
        out_specs=[
            pl.BlockSpec((tm, D_MODEL), row),
            pl.BlockSpec((tm, D_MODEL), row),
            pl.BlockSpec((N_PAIRS, tm), col),
            pl.BlockSpec((N_PAIRS, tm), col),
            pl.BlockSpec((N_PAIRS, tm), col),
        ],
        out_shape=[
            jax.ShapeDtypeStruct((T_ALL, D_MODEL), f32),
            jax.ShapeDtypeStruct((T_ALL, D_MODEL), f32),
            jax.ShapeDtypeStruct((N_PAIRS, T_ALL), jnp.int32),
            jax.ShapeDtypeStruct((N_PAIRS, T_ALL), jnp.int32),
            jax.ShapeDtypeStruct((N_PAIRS, T_ALL), f32),
        ],
        scratch_shapes=[pltpu.VMEM((tm, 2 * PEER_HEADS * PEER_DK), f32)],
        compiler_params=pltpu.CompilerParams(dimension_semantics=("arbitrary",)),
        name="post_router",
    )(x, o, mod3, wo, g, wq, keys)


def _unpack_expert(tab_ref, r8, sh):
    word = tab_ref[pl.ds(pl.multiple_of(r8, SUBLANES), SUBLANES), :]
    return pltpu.bitcast((word << sh.astype(jnp.uint32)) & jnp.uint32(0xFFFF0000), f32)


def _sublane_sums(ps):
    sub = lax.broadcasted_iota(jnp.int32, (SUBLANES, LANES), 0)
    m4 = sub < 4
    m2 = (sub & 2) == 0
    m1 = (sub & 1) == 0
    lvl1 = [jnp.where(m4, ps[a], ps[b]) + pltpu.roll(jnp.where(m4, ps[b], ps[a]), 4, 0)
            for a, b in ((0, 4), (2, 6), (1, 5), (3, 7))]
    lvl2 = [jnp.where(m2, x, pltpu.roll(y, 2, 0)) + jnp.where(m2, pltpu.roll(x, 6, 0), y)
            for x, y in ((lvl1[0], lvl1[1]), (lvl1[2], lvl1[3]))]
    x, y = lvl2
    return jnp.where(m1, x, pltpu.roll(y, 1, 0)) + jnp.where(m1, pltpu.roll(x, 7, 0), y)


def _expert_u_kernel(r8_ref, sh_ref, x_ref, gate_ref, tab_ref, fsel_ref, w_ref, part_scr):
    def token(t, carry):
        xt = x_ref[t]
        lanes = pl.ds(pl.multiple_of(t * LANES, LANES), LANES)
        for j in range(N_PAIRS // SUBLANES):
            ps = [_unpack_expert(tab_ref, r8_ref[t, j * SUBLANES + s], sh_ref[t, j * SUBLANES + s]) * xt
                  for s in range(SUBLANES)]
            part_scr[j * SUBLANES:(j + 1) * SUBLANES, lanes] = _sublane_sums(ps)
        return carry

    lax.fori_loop(0, TB, token, 0)
    hi, lo = _split(part_scr[...])
    fsel = fsel_ref[...]
    a = _dot(hi, fsel) + _dot(lo, fsel)
    gelu = 0.5 * a * (1.0 + lax.erf(a * np.float32(2.0 ** -0.5)))
    w_ref[...] = gelu * gate_ref[...]


def expert_u(r8_t, sh_t, x3, gate, tab, fsel):
    tok = lambda i: (i, 0)
    col = lambda i: (0, i)
    return pl.pallas_call(
        _expert_u_kernel,
        grid=(T_ALL // TB,),
        in_specs=[
            pl.BlockSpec((TB, N_PAIRS), tok, memory_space=pltpu.SMEM),
            pl.BlockSpec((TB, N_PAIRS), tok, memory_space=pltpu.SMEM),
            pl.BlockSpec((TB, SUBLANES, LANES), lambda i: (i, 0, 0)),
            pl.BlockSpec((N_PAIRS, TB), col),
            pl.BlockSpec(memory_space=pltpu.VMEM),
            pl.BlockSpec(memory_space=pltpu.VMEM),
        ],
        out_specs=pl.BlockSpec((N_PAIRS, TB), col),
        out_shape=jax.ShapeDtypeStruct((N_PAIRS, T_ALL), f32),
        scratch_shapes=[pltpu.VMEM((N_PAIRS, TB * LANES), f32)],
        compiler_params=pltpu.CompilerParams(
            dimension_semantics=("arbitrary",), vmem_limit_bytes=VMEM_LIMIT_EXPERT),
        name="expert_u",
    )(r8_t, sh_t, x3, gate, tab, fsel)


def _expert_v_kernel(r8_ref, sh_ref, w_ref, x1_ref, g2_ref, tab_ref, o_ref, wb_scr):
    w = w_ref[...]
    for t in range(TB):
        wb_scr[t] = jnp.broadcast_to(w[:, t:t + 1], (N_PAIRS, LANES))
    g2 = g2_ref[0]

    def token(t, carry):
        accs = [jnp.zeros((SUBLANES, LANES), f32) for _ in range(4)]
        for k in range(N_PAIRS):
            val = _unpack_expert(tab_ref, r8_ref[t, k], sh_ref[t, k])
            wk = jnp.broadcast_to(wb_scr[t, k:k + 1, :], (SUBLANES, LANES))
            accs[k % 4] = accs[k % 4] + wk * val
        o_ref[t] = x1_ref[t] + g2 * ((accs[0] + accs[1]) + (accs[2] + accs[3]))
        return carry

    lax.fori_loop(0, TB, token, 0)


def expert_v(r8_t, sh_t, w, x1_3, g2_3, tab):
    tok = lambda i: (i, 0)
    tok3 = lambda i: (i, 0, 0)
    return pl.pallas_call(
        _expert_v_kernel,
        grid=(T_ALL // TB,),
        in_specs=[
            pl.BlockSpec((TB, N_PAIRS), tok, memory_space=pltpu.SMEM),
            pl.BlockSpec((TB, N_PAIRS), tok, memory_space=pltpu.SMEM),
            pl.BlockSpec((N_PAIRS, TB), lambda i: (0, i)),
            pl.BlockSpec((TB, SUBLANES, LANES), tok3),
            pl.BlockSpec((1, SUBLANES, LANES), _mod_row_map(TB)),
            pl.BlockSpec(memory_space=pltpu.VMEM),
        ],
        out_specs=pl.BlockSpec((TB, SUBLANES, LANES), tok3),
        out_shape=jax.ShapeDtypeStruct((T_ALL, SUBLANES, LANES), f32),
        scratch_shapes=[pltpu.VMEM((TB, N_PAIRS, LANES), f32)],
        compiler_params=pltpu.CompilerParams(
            dimension_semantics=("arbitrary",), vmem_limit_bytes=VMEM_LIMIT_EXPERT),
        name="expert_v",
    )(r8_t, sh_t, w, x1_3, g2_3, tab)


def _final_norm_kernel(x_ref, g_ref, o_ref):
    o_ref[...] = _rmsnorm(x_ref[...], g_ref[...])


def final_norm(x, g):
    tm = TM_QKV
    return pl.pallas_call(
        _final_norm_kernel,
        grid=(T_ALL // tm,),
        in_specs=[pl.BlockSpec((tm, D_MODEL), lambda i: (i, 0)), pl.BlockSpec((1, D_MODEL), lambda i: (0, 0))],
        out_specs=pl.BlockSpec((tm, D_MODEL), lambda i: (i, 0)),
        out_shape=jax.ShapeDtypeStruct((T_ALL, D_MODEL), f32),
        compiler_params=pltpu.CompilerParams(dimension_semantics=("arbitrary",)),
        name="final_norm",
    )(x, g)


def _rope_tables():
    pos = np.arange(DEC_SEQ)
    d = np.arange(LANES) % HEAD_DIM
    quarter = HEAD_DIM // 4
    freq = ROPE_THETA ** (-(d % quarter).astype(np.float64) / quarter)
    coord = np.where((d < HEAD_DIM // 2)[None, :], (pos // GRID_W)[:, None], (pos % GRID_W)[:, None])
    ang = coord.astype(np.float64) * freq[None, :]
    first = ((d % (HEAD_DIM // 2)) < quarter)[None, :]
    cos = np.cos(ang)
    sin = np.sin(ang)
    return (jnp.asarray(cos, f32), jnp.asarray(np.where(first, -sin, 0.0), f32),
            jnp.asarray(np.where(first, 0.0, sin), f32))


def _na_bias_pairs(rpb):
    col = np.arange(GRID_W)
    col_start = np.clip(col - NA_KW // 2, 0, GRID_W - NA_KW)
    c2 = np.arange(GRID_W)
    inside = (c2[None, :] >= col_start[:, None]) & (c2[None, :] < col_start[:, None] + NA_KW)
    dc = np.clip(c2[None, :] - col[:, None] + (NA_KW - 1), 0, 2 * NA_KW - 2)
    full = jnp.where(jnp.asarray(inside)[None, None], rpb[:, :, dc], NEG_INF)
    return jnp.concatenate([full[:, :-1], full[:, 1:]], axis=-1)


def _pack_table_kernel(lo_ref, hi_ref, o_ref):
    lo = pltpu.bitcast(lo_ref[...].astype(bf16).astype(f32), jnp.uint32)
    hi = pltpu.bitcast(hi_ref[...].astype(bf16).astype(f32), jnp.uint32)
    o_ref[...] = (lo >> 16) | (hi & jnp.uint32(0xFFFF0000))


def _pack_table(t):
    rows = 256
    n_blocks = HALF_EXPERTS // rows
    packed = pl.pallas_call(
        _pack_table_kernel,
        grid=(n_blocks,),
        in_specs=[pl.BlockSpec((rows, D_MODEL), lambda i: (i, 0)),
                  pl.BlockSpec((rows, D_MODEL), lambda i: (i + n_blocks, 0))],
        out_specs=pl.BlockSpec((rows, D_MODEL), lambda i: (i, 0)),
        out_shape=jax.ShapeDtypeStruct((HALF_EXPERTS, D_MODEL), jnp.uint32),
        compiler_params=pltpu.CompilerParams(dimension_semantics=("arbitrary",)),
        name="pack_table",
    )(t, t)
    return packed.reshape(HALF_EXPERTS * SUBLANES, LANES)


def kernel(x_prompt, x_sample, cache_k, cache_v, c, c_ctx, w_ada, b_ada, norm_g, final_g, w_qkv, w_o, na_rpb, swa_sink, qk_gain, peer_wq, peer_keys, peer_u, peer_v):
    cvec = jnp.concatenate([c_ctx[None], c, jnp.zeros((N_MOD_ROWS - 1 - DEC_BATCH, D_MODEL), f32)], axis=0)
    mods = ada_mods(cvec, w_ada, b_ada)
    x = jnp.concatenate([x_prompt.reshape(T_CTX, D_MODEL), x_sample.reshape(T_DEC, D_MODEL)], axis=0)

    rope_tabs = _rope_tables()
    head_sum = jnp.asarray(np.kron(np.eye(N_HEADS), np.ones((HEAD_DIM, HEAD_DIM))), bf16)
    fsel = jnp.asarray(np.kron(np.eye(TB), np.ones((LANES, 1))), bf16)

    ks, vs = [], []
    for l in range(DEPTH):
        kind, j = l % 3, l // 3
        mod3 = mods[l].reshape(N_MOD_ROWS, 1, 6 * D_MODEL)
        norm_args = None
        if kind == 2:
            norm_args = (head_sum, jnp.tile(qk_gain[j, 0], N_HEADS)[None], jnp.tile(qk_gain[j, 1], N_KV_HEADS)[None])
        rope_args = rope_tabs if kind != 0 else None
        qb, k, v, kb, vb = qkv_project(x, mod3, norm_g[l, 0][None], w_qkv[l].astype(bf16), norm_args, rope_args)
        ks.append(k[:T_CTX].reshape(BATCH, SEQ, N_KV_HEADS, HEAD_DIM))
        vs.append(v[:T_CTX].reshape(BATCH, SEQ, N_KV_HEADS, HEAD_DIM))

        ck = cache_k[:, l].reshape(DEC_BATCH, PAST_LEN, KV_DIM).astype(bf16)
        cv = cache_v[:, l].reshape(DEC_BATCH, PAST_LEN, KV_DIM).astype(bf16)
        o_ctx = ctx_attention(qb, kb, vb, swa_sink[j] if kind == 1 else None)
        if kind == 0:
            o_dec = na_attention(qb, kb, vb, ck, cv, _na_bias_pairs(na_rpb[j]))
        elif kind == 1:
            o_dec = swa_attention(qb, kb, vb, ck, cv, swa_sink[j])
        else:
            o_dec = dense_attention(qb, kb, vb, ck, cv)
        o = jnp.concatenate([o_ctx, o_dec], axis=0)

        x1, h2, r8, sh, gate = post_router(
            x, o, mod3, w_o[l].astype(bf16), norm_g[l, 1][None], peer_wq[l].astype(bf16),
            peer_keys[l].reshape(2 * PEER_HEADS, PEER_N_KEYS, PEER_DK).astype(bf16))
        r8_t, sh_t = r8.T, sh.T
        w = expert_u(r8_t, sh_t, h2.reshape(T_ALL, SUBLANES, LANES), gate, _pack_table(peer_u[l]), fsel)
        g2 = mods[l][:, 5 * D_MODEL:].reshape(N_MOD_ROWS, SUBLANES, LANES)
        x = expert_v(r8_t, sh_t, w, x1.reshape(T_ALL, SUBLANES, LANES), g2,
                     _pack_table(peer_v[l])).reshape(T_ALL, D_MODEL)

    y = final_norm(x, final_g[None])
    y_prompt = y[:T_CTX].reshape(BATCH, SEQ, D_MODEL)
    y_sample = y[T_CTX:].reshape(DEC_BATCH, DEC_SEQ, D_MODEL)
    return (y_prompt, y_sample, jnp.stack(ks, axis=1), jnp.stack(vs, axis=1))
```
